```python
import jax, jax.numpy as jnp
from jax import lax
import numpy as np

D_MODEL = 1024
BATCH = 8
SEQ = 2048
DEPTH = 1
DEC_BATCH = 128
DEC_SEQ = 4
PAST_LEN = 16384
PAGE_SIZE = 128

C_CONV = D_MODEL
CONF_KERNEL = 31
EXPAND = 2
D_INNER = EXPAND * D_MODEL
HEADDIM = 64
N_HEADS_SSM = D_INNER // HEADDIM
N_GROUPS = 8
HEADS_PER_GROUP = N_HEADS_SSM // N_GROUPS
D_STATE = 128
SSM_CONV = 4
CONV_DIM = D_INNER + 2 * N_GROUPS * D_STATE
CHUNK = 128
D_FF = 4 * D_MODEL
EPS = 1e-6
OFF_CONF = 0
OFF_GATE = OFF_CONF + 2 * C_CONV
OFF_Z = OFF_GATE + 2 * D_MODEL
OFF_XBC = OFF_Z + D_INNER
OFF_DT = OFF_XBC + CONV_DIM
IN_COLS = OFF_DT + N_HEADS_SSM

kernel_name = "hybrid_conformer_ssd_decoder_step"


def rmsnorm(x, g):
    xf = x.astype(jnp.float32)
    y = xf * lax.rsqrt(jnp.mean(xf * xf, axis=-1, keepdims=True) + EPS)
    return (y * g.astype(jnp.float32)).astype(x.dtype)


def layernorm(x, g, b):
    xf = x.astype(jnp.float32)
    mu = jnp.mean(xf, axis=-1, keepdims=True)
    xc = xf - mu
    y = xc * lax.rsqrt(jnp.mean(xc * xc, axis=-1, keepdims=True) + EPS)
    return (y * g.astype(jnp.float32) + b.astype(jnp.float32)).astype(x.dtype)


def causal_dwconv(x, prefix, w, b):
    k = w.shape[0]
    xe = jnp.concatenate([prefix.astype(x.dtype), x], axis=1)
    y = lax.conv_general_dilated(
        xe, w[:, None, :].astype(x.dtype), window_strides=(1,), padding="VALID",
        dimension_numbers=("NWC", "WIO", "NWC"), feature_group_count=x.shape[-1])
    new_prefix = xe[:, xe.shape[1] - (k - 1):]
    return y + b.astype(x.dtype), new_prefix


def ssd_scan(x, dt, a_neg, bm, cm, h0):
    b, l = x.shape[0], x.shape[1]
    q = CHUNK if l % CHUNK == 0 else l
    nc = l // q
    f32 = jnp.float32
    G, E, P, N = N_GROUPS, HEADS_PER_GROUP, HEADDIM, D_STATE
    dtg = dt.astype(f32).reshape(b, nc, q, G, E)
    xdt = x.astype(f32).reshape(b, nc, q, G, E, P) * dtg[..., None]
    a = dtg * a_neg.astype(f32).reshape(G, E)
    bc = bm.astype(f32).reshape(b, nc, q, G, N)
    cc = cm.astype(f32).reshape(b, nc, q, G, N)
    xs = (jnp.moveaxis(xdt, 1, 0), jnp.moveaxis(a, 1, 0), jnp.moveaxis(bc, 1, 0), jnp.moveaxis(cc, 1, 0))
    causal = jnp.tril(jnp.ones((q, q), dtype=bool))[None, :, :, None, None]

    def step(h, inp):
        xdt_c, a_c, b_c, c_c = inp
        acum = jnp.cumsum(a_c, axis=1)
        seg = acum[:, :, None] - acum[:, None, :]
        decay = jnp.exp(jnp.where(causal, seg, -jnp.inf))
        scores = jnp.einsum('blgn,bsgn->blsg', c_c, b_c)
        y_diag = jnp.einsum('blsge,bsgep->blgep', scores[..., None] * decay, xdt_c)
        y_off = jnp.einsum('blgn,bgepn->blgep', c_c, h) * jnp.exp(acum)[..., None]
        a_last = acum[:, -1]
        w_s = jnp.exp(a_last[:, None] - acum)
        h_new = h * jnp.exp(a_last)[..., None, None] + jnp.einsum('bsgn,bsge,bsgep->bgepn', b_c, w_s, xdt_c)
        return h_new, y_diag + y_off

    h_init = h0.astype(f32).reshape(b, G, E, P, N)
    h_fin, ys = lax.scan(step, h_init, xs)
    y = jnp.moveaxis(ys, 0, 1).reshape(b, l, N_HEADS_SSM, P)
    return y.astype(x.dtype), h_fin.reshape(b, N_HEADS_SSM, P, N).astype(h0.dtype)


def hybrid_layer(x, conf_buf, ssm_buf, ssm_h, g_mix, w_in, conf_dw_w, conf_dw_b, conf_ln_g, conf_ln_b,
                 conf_w_pw, conf_b_pw, ssm_conv_w, ssm_conv_b, ssm_dt_bias, ssm_a_log, ssm_d,
                 ssm_norm_g, ssm_w_out, w_out, g_ffn, w_up, w_down):
    b, l, _ = x.shape
    u = rmsnorm(x, g_mix)
    proj = u @ w_in
    conf_in = proj[..., OFF_CONF:OFF_GATE]
    gates = proj[..., OFF_GATE:OFF_Z]
    z = proj[..., OFF_Z:OFF_XBC]
    xbc = proj[..., OFF_XBC:OFF_DT]
    dt_raw = proj[..., OFF_DT:IN_COLS]

    glu = conf_in[..., :C_CONV] * jax.nn.sigmoid(conf_in[..., C_CONV:])
    cconv, new_conf_buf = causal_dwconv(glu, conf_buf, conf_dw_w, conf_dw_b)
    cconv = jax.nn.silu(layernorm(cconv, conf_ln_g, conf_ln_b))
    branch_a = cconv @ conf_w_pw + conf_b_pw

    xbc_c, new_ssm_buf = causal_dwconv(xbc, ssm_buf, ssm_conv_w, ssm_conv_b)
    xbc_c = jax.nn.silu(xbc_c)
    xs = xbc_c[..., :D_INNER].reshape(b, l, N_HEADS_SSM, HEADDIM)
    bm = xbc_c[..., D_INNER:D_INNER + N_GROUPS * D_STATE].reshape(b, l, N_GROUPS, D_STATE)
    cm = xbc_c[..., D_INNER + N_GROUPS * D_STATE:].reshape(b, l, N_GROUPS, D_STATE)
    dt = jax.nn.softplus(dt_raw.astype(jnp.float32) + ssm_dt_bias.astype(jnp.float32))
    a_neg = -jnp.exp(ssm_a_log.astype(jnp.float32))
    y, new_h = ssd_scan(xs, dt, a_neg, bm, cm, ssm_h)
    y = y + xs * ssm_d[:, None].astype(xs.dtype)
    y = y.reshape(b, l, D_INNER) * jax.nn.silu(z)
    y = rmsnorm(y.reshape(b, l, N_GROUPS, D_INNER // N_GROUPS),
                ssm_norm_g.reshape(N_GROUPS, D_INNER // N_GROUPS)).reshape(b, l, D_INNER)
    branch_b = y @ ssm_w_out

    merged = jax.nn.sigmoid(gates[..., :D_MODEL]) * branch_a + jax.nn.sigmoid(gates[..., D_MODEL:]) * branch_b
    h = x + merged @ w_out
    hidden = jnp.square(jax.nn.relu(rmsnorm(h, g_ffn) @ w_up))
    h = h + hidden @ w_down
    return h, new_conf_buf, new_ssm_buf, new_h


def setup_inputs(seed: int = 0) -> dict:
    key = jax.random.key(seed)
    ks = jax.random.split(key, 32)
    f32 = jnp.float32

    def nrm(k, shape, scale):
        return jax.random.normal(k, shape, f32) * scale

    def gain(k, shape):
        return 1.0 + 0.05 * jax.random.normal(k, shape, f32)

    dt0 = jnp.exp(jax.random.uniform(ks[15], (DEPTH, N_HEADS_SSM), f32, np.log(1e-3), np.log(1e-1)))
    dt_bias = dt0 + jnp.log(-jnp.expm1(-dt0))
    return {
        "x_prompt": nrm(ks[0], (BATCH, SEQ, D_MODEL), 1.0),
        "x_sample": nrm(ks[1], (DEC_BATCH, DEC_SEQ, D_MODEL), 1.0),
        "state_conf_conv": nrm(ks[2], (DEPTH, DEC_BATCH, CONF_KERNEL - 1, C_CONV), 1.0),
        "state_ssm_conv": nrm(ks[3], (DEPTH, DEC_BATCH, SSM_CONV - 1, CONV_DIM), 1.0),
        "state_ssm": nrm(ks[4], (DEPTH, DEC_BATCH, N_HEADS_SSM, HEADDIM, D_STATE), 0.5),
        "g_mix": gain(ks[5], (DEPTH, D_MODEL)),
        "w_in": nrm(ks[6], (DEPTH, D_MODEL, IN_COLS), D_MODEL ** -0.5),
        "conf_dw_w": nrm(ks[7], (DEPTH, CONF_KERNEL, C_CONV), CONF_KERNEL ** -0.5),
        "conf_dw_b": nrm(ks[8], (DEPTH, C_CONV), 0.02),
        "conf_ln_g": gain(ks[9], (DEPTH, C_CONV)),
        "conf_ln_b": nrm(ks[10], (DEPTH, C_CONV), 0.02),
        "conf_w_pw": nrm(ks[11], (DEPTH, C_CONV, D_MODEL), C_CONV ** -0.5),
        "conf_b_pw": nrm(ks[12], (DEPTH, D_MODEL), 0.02),
        "ssm_conv_w": nrm(ks[13], (DEPTH, SSM_CONV, CONV_DIM), SSM_CONV ** -0.5),
        "ssm_conv_b": nrm(ks[14], (DEPTH, CONV_DIM), 0.02),
        "ssm_dt_bias": dt_bias,
        "ssm_a_log": jnp.log(jax.random.uniform(ks[16], (DEPTH, N_HEADS_SSM), f32, 1.0, 16.0)),
        "ssm_d": gain(ks[17], (DEPTH, N_HEADS_SSM)),
        "ssm_norm_g": gain(ks[18], (DEPTH, D_INNER)),
        "ssm_w_out": nrm(ks[19], (DEPTH, D_INNER, D_MODEL), D_INNER ** -0.5),
        "w_out": nrm(ks[20], (DEPTH, D_MODEL, D_MODEL), D_MODEL ** -0.5),
        "g_ffn": gain(ks[21], (DEPTH, D_MODEL)),
        "w_up": nrm(ks[22], (DEPTH, D_MODEL, D_FF), D_MODEL ** -0.5),
        "w_down": nrm(ks[23], (DEPTH, D_FF, D_MODEL), D_FF ** -0.5),
        "g_final": gain(ks[24], (D_MODEL,)),
    }


def reference(x_prompt, x_sample, state_conf_conv, state_ssm_conv, state_ssm, g_mix, w_in, conf_dw_w,
              conf_dw_b, conf_ln_g, conf_ln_b, conf_w_pw, conf_b_pw, ssm_conv_w, ssm_conv_b, ssm_dt_bias,
              ssm_a_log, ssm_d, ssm_norm_g, ssm_w_out, w_out, g_ffn, w_up, w_down, g_final):
    hp, hs = x_prompt, x_sample
    pc_list, ps_list, ph_list, sc_list, ss_list, sh_list = [], [], [], [], [], []
    for i in range(DEPTH):
        lw = (g_mix[i], w_in[i], conf_dw_w[i], conf_dw_b[i], conf_ln_g[i], conf_ln_b[i], conf_w_pw[i],
              conf_b_pw[i], ssm_conv_w[i], ssm_conv_b[i], ssm_dt_bias[i], ssm_a_log[i], ssm_d[i],
              ssm_norm_g[i], ssm_w_out[i], w_out[i], g_ffn[i], w_up[i], w_down[i])
        pc0 = jnp.zeros((BATCH, CONF_KERNEL - 1, C_CONV), x_prompt.dtype)
        ps0 = jnp.zeros((BATCH, SSM_CONV - 1, CONV_DIM), x_prompt.dtype)
        ph0 = jnp.zeros((BATCH, N_HEADS_SSM, HEADDIM, D_STATE), x_prompt.dtype)
        hp, pc, ps, ph = hybrid_layer(hp, pc0, ps0, ph0, *lw)
        hs, sc, ss, sh = hybrid_layer(hs, state_conf_conv[i], state_ssm_conv[i], state_ssm[i], *lw)
        pc_list.append(pc); ps_list.append(ps); ph_list.append(ph)
        sc_list.append(sc); ss_list.append(ss); sh_list.append(sh)
    y_prompt = rmsnorm(hp, g_final)
    y_sample = rmsnorm(hs, g_final)
    new_conf_conv_prompt = jnp.stack(pc_list)
    new_ssm_conv_prompt = jnp.stack(ps_list)
    new_ssm_prompt = jnp.stack(ph_list)
    new_conf_conv_sample = jnp.stack(sc_list)
    new_ssm_conv_sample = jnp.stack(ss_list)
    new_ssm_sample = jnp.stack(sh_list)
    return (y_prompt, y_sample, new_conf_conv_prompt, new_ssm_conv_prompt, new_ssm_prompt,
            new_conf_conv_sample, new_ssm_conv_sample, new_ssm_sample)
```

```python
import functools

import jax
import jax.numpy as jnp
from jax import lax
from jax.experimental import pallas as pl
from jax.experimental.pallas import tpu as pltpu

F32 = jnp.float32
BF16 = jnp.bfloat16

D_MODEL = 1024
C_CONV = 1024
CONF_KERNEL = 31
CONF_HALO = CONF_KERNEL - 1
D_INNER = 2048
HEADDIM = 64
N_HEADS = 32
N_GROUPS = 8
HEADS_PER_GROUP = 4
GROUP_W = HEADS_PER_GROUP * HEADDIM
D_STATE = 128
SSM_CONV = 4
SSM_HALO = SSM_CONV - 1
CONV_DIM = 4096
CHUNK = 128
D_FF = 4096
EPS = 1e-6
MAIN_COLS = 10240
DT_PAD = 128
LANES = 128
VMEM_LIMIT = 56 * 1024 * 1024


def _sigmoid(x):
    return jax.nn.sigmoid(x)


def _silu(x):
    return x * jax.nn.sigmoid(x)


def _softplus(x):
    return jnp.maximum(x, 0.0) + jnp.log1p(jnp.exp(-jnp.abs(x)))


def _rmsnorm(x, g):
    return x * lax.rsqrt(jnp.mean(x * x, axis=-1, keepdims=True) + EPS) * g


def _dot(a, b):
    return jnp.dot(a, b, preferred_element_type=F32)


def _dot_nt(a, b):
    return lax.dot_general(a, b, (((1,), (1,)), ((), ())), preferred_element_type=F32)


def _dot_tn(a, b):
    return lax.dot_general(a, b, (((0,), (0,)), ((), ())), preferred_element_type=F32)


def _split3(x):
    hi = x.astype(BF16)
    r1 = x - hi.astype(F32)
    mid = r1.astype(BF16)
    lo = (r1 - mid.astype(F32)).astype(BF16)
    return hi, mid, lo


def _cumsum_rows(a, tri):
    hi, mid, lo = _split3(a)
    return _dot(tri, hi) + _dot(tri, mid) + _dot(tri, lo)


def _inproj_kernel(x_ref, g_ref, w_ref, wdt_ref, bdt_ref, proj_ref, dt_ref, u_scr):
    @pl.when(pl.program_id(1) == 0)
    def _():
        ub = _rmsnorm(x_ref[...], g_ref[...]).astype(BF16)
        u_scr[...] = ub
        dt_ref[...] = _softplus(_dot(ub, wdt_ref[...]) + bdt_ref[...])

    proj_ref[...] = _dot(u_scr[...], w_ref[...])


def _inproj(x2d, g_mix, w_main, w_dt, b_dt, tm, tn=1024):
    m = x2d.shape[0]
    return pl.pallas_call(
        _inproj_kernel,
        grid=(m // tm, MAIN_COLS // tn),
        in_specs=[
            pl.BlockSpec((tm, D_MODEL), lambda i, j: (i, 0)),
            pl.BlockSpec((1, D_MODEL), lambda i, j: (0, 0)),
            pl.BlockSpec((D_MODEL, tn), lambda i, j: (0, j)),
            pl.BlockSpec((D_MODEL, DT_PAD), lambda i, j: (0, 0)),
            pl.BlockSpec((1, DT_PAD), lambda i, j: (0, 0)),
        ],
        out_specs=[
            pl.BlockSpec((tm, tn), lambda i, j: (i, j)),
            pl.BlockSpec((tm, DT_PAD), lambda i, j: (i, 0)),
        ],
        out_shape=[
            jax.ShapeDtypeStruct((m, MAIN_COLS), F32),
            jax.ShapeDtypeStruct((m, DT_PAD), F32),
        ],
        scratch_shapes=[pltpu.VMEM((tm, D_MODEL), BF16)],
        compiler_params=pltpu.CompilerParams(
            dimension_semantics=("arbitrary", "arbitrary"), vmem_limit_bytes=VMEM_LIMIT),
        name="inproj",
    )(x2d, g_mix, w_main, w_dt, b_dt)


def _ln_swish(y, g, b):
    mu = jnp.mean(y, axis=-1, keepdims=True)
    yc = y - mu
    yn = yc * lax.rsqrt(jnp.mean(yc * yc, axis=-1, keepdims=True) + EPS) * g + b
    return _silu(yn)


CONVA_T = 256
CONVA_RB = 32
CONVA_CW = 256
CONVA_PAD = 32


def _conva_prompt_kernel(a_ref, b_ref, w_ref, bias_ref, lng_ref, lnb_ref, ca_ref, st_ref, xe, conv):
    t_tile = CONVA_T

    @pl.when(pl.program_id(1) == 0)
    def _():
        xe[0:CONVA_PAD, :] = jnp.zeros((CONVA_PAD, C_CONV), F32)

    xe[CONVA_PAD:CONVA_PAD + t_tile, :] = a_ref[0] * _sigmoid(b_ref[0])
    off = CONVA_PAD - CONF_HALO
    for rb in range(t_tile // CONVA_RB):
        r0 = rb * CONVA_RB
        for c in range(C_CONV // CONVA_CW):
            cs = slice(c * CONVA_CW, (c + 1) * CONVA_CW)
            acc = jnp.broadcast_to(bias_ref[:, cs], (CONVA_RB, CONVA_CW))
            for k in range(CONF_KERNEL):
                acc = acc + w_ref[k:k + 1, cs] * xe[r0 + off + k:r0 + off + k + CONVA_RB, cs]
            conv[r0:r0 + CONVA_RB, cs] = acc
    ca_ref[0] = _ln_swish(conv[...], lng_ref[...], lnb_ref[...]).astype(BF16)
    st_ref[0] = xe[t_tile + off:t_tile + CONVA_PAD, :]
    xe[0:CONVA_PAD, :] = xe[t_tile:t_tile + CONVA_PAD, :]


def _conva_prompt(proj3, dw_w, dw_b, ln_g, ln_b):
    nb, seq, _ = proj3.shape
    t = CONVA_T
    vec = pl.BlockSpec((1, C_CONV), lambda b, i: (0, 0))
    return pl.pallas_call(
        _conva_prompt_kernel,
        grid=(nb, seq // t),
        in_specs=[
            pl.BlockSpec((1, t, C_CONV), lambda b, i: (b, i, 0)),
            pl.BlockSpec((1, t, C_CONV), lambda b, i: (b, i, 1)),
            pl.BlockSpec((CONF_KERNEL, C_CONV), lambda b, i: (0, 0)),
            vec, vec, vec,
        ],
        out_specs=[
            pl.BlockSpec((1, t, C_CONV), lambda b, i: (b, i, 0)),
            pl.BlockSpec((1, CONF_HALO, C_CONV), lambda b, i: (b, 0, 0)),
        ],
        out_shape=[
            jax.ShapeDtypeStruct((nb, seq, C_CONV), BF16),
            jax.ShapeDtypeStruct((nb, CONF_HALO, C_CONV), F32),
        ],
        scratch_shapes=[pltpu.VMEM((t + CONVA_PAD, C_CONV), F32), pltpu.VMEM((t, C_CONV), F32)],
        compiler_params=pltpu.CompilerParams(
            dimension_semantics=("arbitrary", "arbitrary"), vmem_limit_bytes=VMEM_LIMIT),
        name="conva_prompt",
    )(proj3, proj3, dw_w, dw_b, ln_g, ln_b)


CONVA_SR = 32


def _conva_sample_kernel(a_ref, b_ref, st_ref, w_ref, bias_ref, lng_ref, lnb_ref,
                         ca_ref, nst_ref, xe, conv):
    nreq, steps = a_ref.shape[0], a_ref.shape[1]
    xe[32:40, :] = jnp.zeros((8, C_CONV), F32)

    per = 8 // steps

    def body(p, carry):
        rows = []
        for q in range(per):
            r = p * per + q
            xe[0:CONF_HALO, :] = st_ref[r]
            xe[CONF_HALO:CONF_HALO + steps, :] = a_ref[r] * _sigmoid(b_ref[r])
            nst_ref[r] = xe[steps:steps + CONF_HALO, :]
            rows += [jnp.sum(w_ref[...] * xe[t:t + 32, :], axis=0, keepdims=True) for t in range(steps)]
        conv[pl.ds(pl.multiple_of(p * 8, 8), 8), :] = jnp.concatenate(rows, axis=0) + bias_ref[...]
        return carry

    lax.fori_loop(0, nreq // per, body, 0)
    ca_ref[...] = _ln_swish(conv[...], lng_ref[...], lnb_ref[...]).astype(BF16)


def _conva_sample(proj3, state, dw_w32, dw_b, ln_g, ln_b):
    nreq, steps, _ = proj3.shape
    r = CONVA_SR
    vec = pl.BlockSpec((1, C_CONV), lambda i: (0, 0))
    return pl.pallas_call(
        _conva_sample_kernel,
        grid=(nreq // r,),
        in_specs=[
            pl.BlockSpec((r, steps, C_CONV), lambda i: (i, 0, 0)),
            pl.BlockSpec((r, steps, C_CONV), lambda i: (i, 0, 1)),
            pl.BlockSpec((r, CONF_HALO, C_CONV), lambda i: (i, 0, 0)),
            pl.BlockSpec((32, C_CONV), lambda i: (0, 0)),
            vec, vec, vec,
        ],
        out_specs=[
            pl.BlockSpec((r * steps, C_CONV), lambda i: (i, 0)),
            pl.BlockSpec((r, CONF_HALO, C_CONV), lambda i: (i, 0, 0)),
        ],
        out_shape=[
            jax.ShapeDtypeStruct((nreq * steps, C_CONV), BF16),
            jax.ShapeDtypeStruct((nreq, CONF_HALO, C_CONV), F32),
        ],
        scratch_shapes=[pltpu.VMEM((40, C_CONV), F32), pltpu.VMEM((r * steps, C_CONV), F32)],
        compiler_params=pltpu.CompilerParams(
            dimension_semantics=("arbitrary",), vmem_limit_bytes=VMEM_LIMIT),
        name="conva_sample",
    )(proj3, proj3, state, dw_w32, dw_b, ln_g, ln_b)


def _expand_heads(v, g, rows):
    lane = lax.broadcasted_iota(jnp.int32, (rows, GROUP_W), 1)
    out = jnp.broadcast_to(v[:, 4 * g + 3:4 * g + 4], (rows, GROUP_W))
    for e in (2, 1, 0):
        col = jnp.broadcast_to(v[:, 4 * g + e:4 * g + e + 1], (rows, GROUP_W))
        out = jnp.where(lane < (e + 1) * HEADDIM, col, out)
    return out


def _gated_group_norm(y, z, g):
    y = y * _silu(z)
    return y * lax.rsqrt(jnp.mean(y * y, axis=-1, keepdims=True) + EPS) * g


SSD_PAD = 8


def _ssd_prompt_kernel(z_ref, xr_ref, bcr_ref, dt_ref, cw_ref, cb_ref, alog_ref, dexp_ref, ng_ref,
                       yn_ref, ncv_ref, h_ref, xe_x, xe_bc):
    t = CHUNK

    @pl.when(pl.program_id(1) == 0)
    def _():
        xe_x[0:SSD_PAD, :] = jnp.zeros((SSD_PAD, D_INNER), F32)
        xe_bc[0:SSD_PAD, :] = jnp.zeros((SSD_PAD, D_INNER), F32)
        h_ref[...] = jnp.zeros(h_ref.shape, F32)

    xe_x[SSD_PAD:SSD_PAD + t, :] = xr_ref[0]
    xe_bc[SSD_PAD:SSD_PAD + t, :] = bcr_ref[0]
    off = SSD_PAD - SSM_HALO
    ncv_ref[0, :, 0:D_INNER] = xe_x[t + off:t + SSD_PAD, :]
    ncv_ref[0, :, D_INNER:CONV_DIM] = xe_bc[t + off:t + SSD_PAD, :]

    def conv(xe, lo, width, wofs):
        acc = jnp.broadcast_to(cb_ref[:, wofs + lo:wofs + lo + width], (t, width))
        for k in range(SSM_CONV):
            acc = acc + cw_ref[k:k + 1, wofs + lo:wofs + lo + width] * xe[off + k:off + k + t, lo:lo + width]
        return _silu(acc)

    row = lax.broadcasted_iota(jnp.int32, (t, t), 0)
    col = lax.broadcasted_iota(jnp.int32, (t, t), 1)
    causal = row >= col
    tri = causal.astype(BF16)

    dt = dt_ref[0]
    a = dt * (-jnp.exp(alog_ref[...]))
    acum = _cumsum_rows(a, tri)
    acum_t = acum.T
    a_last = acum[t - 1:t, :]
    eac = jnp.exp(acum)
    wsd = jnp.exp(a_last - acum)
    ealast = jnp.exp(a_last)

    for g in range(N_GROUPS):
        xs = conv(xe_x, g * GROUP_W, GROUP_W, 0)
        bm = conv(xe_bc, g * D_STATE, D_STATE, D_INNER).astype(BF16)
        cm = conv(xe_bc, N_GROUPS * D_STATE + g * D_STATE, D_STATE, D_INNER).astype(BF16)
        scores = _dot_nt(cm, bm)
        h_g = h_ref[0, 4 * g:4 * g + 4].reshape(GROUP_W, D_STATE)
        xdt = xs * _expand_heads(dt, g, t)
        y = _dot_nt(cm, h_g.astype(BF16)) * _expand_heads(eac, g, t) + xs * dexp_ref[:, g * GROUP_W:(g + 1) * GROUP_W]
        xdt_b = xdt.astype(BF16)
        yd = []
        for e in range(HEADS_PER_GROUP):
            hd = 4 * g + e
            seg = acum[:, hd:hd + 1] - acum_t[hd:hd + 1, :]
            decay = jnp.exp(jnp.where(causal, seg, -jnp.inf))
            m = (scores * decay).astype(BF16)
            yd.append(_dot(m, xdt_b[:, e * HEADDIM:(e + 1) * HEADDIM]))
        y = y + jnp.concatenate(yd, axis=1)
        xw = (xdt * _expand_heads(wsd, g, t)).astype(BF16)
        upd = _dot_tn(xw, bm)
        for e in range(HEADS_PER_GROUP):
            hd = 4 * g + e
            h_ref[0, hd] = h_ref[0, hd] * ealast[:, hd:hd + 1] + upd[e * HEADDIM:(e + 1) * HEADDIM, :]
        gs = slice(g * GROUP_W, (g + 1) * GROUP_W)
        yn_ref[0, :, gs] = _gated_group_norm(y, z_ref[0, :, gs], ng_ref[:, gs]).astype(BF16)

    xe_x[0:SSD_PAD, :] = xe_x[t:t + SSD_PAD, :]
    xe_bc[0:SSD_PAD, :] = xe_bc[t:t + SSD_PAD, :]


def _ssd_prompt(proj3, dt3, conv_w, conv_b, a_log, d_exp, norm_g):
    nb, seq, _ = proj3.shape
    t = CHUNK
    return pl.pallas_call(
        _ssd_prompt_kernel,
        grid=(nb, seq // t),
        in_specs=[
            pl.BlockSpec((1, t, D_INNER), lambda b, c: (b, c, 2)),
            pl.BlockSpec((1, t, D_INNER), lambda b, c: (b, c, 3)),
            pl.BlockSpec((1, t, D_INNER), lambda b, c: (b, c, 4)),
            pl.BlockSpec((1, t, DT_PAD), lambda b, c: (b, c, 0)),
            pl.BlockSpec((SSM_CONV, CONV_DIM), lambda b, c: (0, 0)),
            pl.BlockSpec((1, CONV_DIM), lambda b, c: (0, 0)),
            pl.BlockSpec((1, DT_PAD), lambda b, c: (0, 0)),
            pl.BlockSpec((1, D_INNER), lambda b, c: (0, 0)),
            pl.BlockSpec((1, D_INNER), lambda b, c: (0, 0)),
        ],
        out_specs=[
            pl.BlockSpec((1, t, D_INNER), lambda b, c: (b, c, 0)),
            pl.BlockSpec((1, SSM_HALO, CONV_DIM), lambda b, c: (b, 0, 0)),
            pl.BlockSpec((1, N_HEADS, HEADDIM, D_STATE), lambda b, c: (b, 0, 0, 0)),
        ],
        out_shape=[
            jax.ShapeDtypeStruct((nb, seq, D_INNER), BF16),
            jax.ShapeDtypeStruct((nb, SSM_HALO, CONV_DIM), F32),
            jax.ShapeDtypeStruct((nb, N_HEADS, HEADDIM, D_STATE), F32),
        ],
        scratch_shapes=[pltpu.VMEM((t + SSD_PAD, D_INNER), F32), pltpu.VMEM((t + SSD_PAD, D_INNER), F32)],
        compiler_params=pltpu.CompilerParams(
            dimension_semantics=("arbitrary", "arbitrary"), vmem_limit_bytes=VMEM_LIMIT),
        name="ssd_prompt",
    )(proj3, proj3, proj3, dt3, conv_w, conv_b, a_log, d_exp, norm_g)


SSD_SR = 4
SSD_ST = 8


def _ssd_sample_kernel(z_ref, xr_ref, bcr_ref, dt_ref, cst_ref, h0_ref, cw_ref, cb_ref, alog_ref,
                       dexp_ref, ng_ref, yn_ref, ncv_ref, h_ref, xe, dt8, z8):
    nreq, steps = z_ref.shape[0], z_ref.shape[1]
    t = SSD_ST
    xe[...] = jnp.zeros(xe.shape, F32)
    dt8[...] = jnp.zeros(dt8.shape, F32)
    z8[...] = jnp.zeros(z8.shape, F32)
    rows = lax.broadcasted_iota(jnp.int32, (t, 1), 0)
    live = rows < steps
    a_neg = -jnp.exp(alog_ref[...])

    def body(r, carry):
        xe[0:SSM_HALO, 0:D_INNER] = cst_ref[r, :, 0:D_INNER]
        xe[0:SSM_HALO, D_INNER:CONV_DIM] = cst_ref[r, :, D_INNER:CONV_DIM]
        xe[SSM_HALO:SSM_HALO + steps, 0:D_INNER] = xr_ref[r]
        xe[SSM_HALO:SSM_HALO + steps, D_INNER:CONV_DIM] = bcr_ref[r]
        ncv_ref[r] = xe[steps:steps + SSM_HALO, :]
        dt8[0:steps, :] = dt_ref[r]
        z8[0:steps, :] = z_ref[r]

        def conv(lo, width):
            acc = jnp.broadcast_to(cb_ref[:, lo:lo + width], (t, width))
            for k in range(SSM_CONV):
                acc = acc + cw_ref[k:k + 1, lo:lo + width] * xe[k:k + t, lo:lo + width]
            return jnp.where(live, _silu(acc), 0.0)

        dt = dt8[...]
        a = dt * a_neg
        acum = jnp.zeros((t, DT_PAD), F32)
        for s in range(steps):
            acum = acum + jnp.where(rows >= s, a[s:s + 1, :], 0.0)
        a_last = acum[t - 1:t, :]
        eac = jnp.exp(acum)
        wsd = jnp.exp(a_last - acum)
        ealast = jnp.exp(a_last)

        for g in range(N_GROUPS):
            xs = conv(g * GROUP_W, GROUP_W)
            bm = conv(D_INNER + g * D_STATE, D_STATE)
            cm = conv(D_INNER + N_GROUPS * D_STATE + g * D_STATE, D_STATE)
            scores = _dot_nt(cm, bm)
            h_g = h0_ref[r, 4 * g:4 * g + 4].reshape(GROUP_W, D_STATE)
            xdt = xs * _expand_heads(dt, g, t)
            acx = _expand_heads(acum, g, t)
            y = _dot_nt(cm, h_g) * _expand_heads(eac, g, t) + xs * dexp_ref[:, g * GROUP_W:(g + 1) * GROUP_W]
            for s in range(steps):
                decay = jnp.exp(jnp.where(rows >= s, acx - acx[s:s + 1, :], -jnp.inf))
                y = y + (scores[:, s:s + 1] * decay) * xdt[s:s + 1, :]
            upd = _dot_tn(xdt * _expand_heads(wsd, g, t), bm)
            for e in range(HEADS_PER_GROUP):
                hd = 4 * g + e
                h_ref[r, hd] = h0_ref[r, hd] * ealast[:, hd:hd + 1] + upd[e * HEADDIM:(e + 1) * HEADDIM, :]
            gs = slice(g * GROUP_W, (g + 1) * GROUP_W)
            yn = _gated_group_norm(y, z8[:, gs], ng_ref[:, gs])
            yn_ref[r, :, gs] = yn[0:steps, :]
        return carry

    lax.fori_loop(0, nreq, body, 0)


def _ssd_sample(proj3, dt3, conv_state, h0, conv_w, conv_b, a_log, d_exp, norm_g):
    nreq, steps, _ = proj3.shape
    r = SSD_SR
    return pl.pallas_call(
        _ssd_sample_kernel,
        grid=(nreq // r,),
        in_specs=[
            pl.BlockSpec((r, steps, D_INNER), lambda i: (i, 0, 2)),
            pl.BlockSpec((r, steps, D_INNER), lambda i: (i, 0, 3)),
            pl.BlockSpec((r, steps, D_INNER), lambda i: (i, 0, 4)),
            pl.BlockSpec((r, steps, DT_PAD), lambda i: (i, 0, 0)),
            pl.BlockSpec((r, SSM_HALO, CONV_DIM), lambda i: (i, 0, 0)),
            pl.BlockSpec((r, N_HEADS, HEADDIM, D_STATE), lambda i: (i, 0, 0, 0)),
            pl.BlockSpec((SSM_CONV, CONV_DIM), lambda i: (0, 0)),
            pl.BlockSpec((1, CONV_DIM), lambda i: (0, 0)),
            pl.BlockSpec((1, DT_PAD), lambda i: (0, 0)),
            pl.BlockSpec((1, D_INNER), lambda i: (0, 0)),
            pl.BlockSpec((1, D_INNER), lambda i: (0, 0)),
        ],
        out_specs=[
            pl.BlockSpec((r, steps, D_INNER), lambda i: (i, 0, 0)),
            pl.BlockSpec((r, SSM_HALO, CONV_DIM), lambda i: (i, 0, 0)),
            pl.BlockSpec((r, N_HEADS, HEADDIM, D_STATE), lambda i: (i, 0, 0, 0)),
        ],
        out_shape=[
            jax.ShapeDtypeStruct((nreq, steps, D_INNER), F32),
            jax.ShapeDtypeStruct((nreq, SSM_HALO, CONV_DIM), F32),
            jax.ShapeDtypeStruct((nreq, N_HEADS, HEADDIM, D_STATE), F32),
        ],
        scratch_shapes=[
            pltpu.VMEM((16, CONV_DIM), F32),
            pltpu.VMEM((SSD_ST, DT_PAD), F32),
            pltpu.VMEM((SSD_ST, D_INNER), F32),
        ],
        compiler_params=pltpu.CompilerParams(
            dimension_semantics=("arbitrary",), vmem_limit_bytes=VMEM_LIMIT),
        name="ssd_sample",
    )(proj3, proj3, proj3, dt3, conv_state, h0, conv_w, conv_b, a_log, d_exp, norm_g)


def _tail_kernel(x_ref, ca_ref, yn_ref, gate_ref, wpw_ref, bpw_ref, wssm_ref, wout_ref, gffn_ref,
                 wup_ref, wdown_ref, gfin_ref, y_ref):
    branch_a = _dot(ca_ref[...], wpw_ref[...]) + bpw_ref[...]
    branch_b = _dot(yn_ref[...].astype(BF16), wssm_ref[...])
    merged = (_sigmoid(gate_ref[:, 0:D_MODEL]) * branch_a
              + _sigmoid(gate_ref[:, D_MODEL:2 * D_MODEL]) * branch_b)
    h = x_ref[...] + _dot(merged.astype(BF16), wout_ref[...])
    hidden = jnp.square(jnp.maximum(_dot(_rmsnorm(h, gffn_ref[...]).astype(BF16), wup_ref[...]), 0.0))
    h = h + _dot(hidden.astype(BF16), wdown_ref[...])
    y_ref[...] = _rmsnorm(h, gfin_ref[...])


def _tail(x2d, ca, yn, proj2d, w_pw, b_pw, w_ssm, w_out, g_ffn, w_up, w_down, g_final, tm):
    m = x2d.shape[0]

    def const(shape):
        return pl.BlockSpec(shape, lambda i: (0, 0), pipeline_mode=pl.Buffered(1))

    return pl.pallas_call(
        _tail_kernel,
        grid=(m // tm,),
        in_specs=[
            pl.BlockSpec((tm, D_MODEL), lambda i: (i, 0)),
            pl.BlockSpec((tm, C_CONV), lambda i: (i, 0)),
            pl.BlockSpec((tm, D_INNER), lambda i: (i, 0)),
            pl.BlockSpec((tm, 2 * D_MODEL), lambda i: (i, 1)),
            const((C_CONV, D_MODEL)), const((1, D_MODEL)), const((D_INNER, D_MODEL)),
            const((D_MODEL, D_MODEL)), const((1, D_MODEL)), const((D_MODEL, D_FF)),
            const((D_FF, D_MODEL)), const((1, D_MODEL)),
        ],
        out_specs=pl.BlockSpec((tm, D_MODEL), lambda i: (i, 0)),
        out_shape=jax.ShapeDtypeStruct((m, D_MODEL), F32),
        compiler_params=pltpu.CompilerParams(
            dimension_semantics=("arbitrary",), vmem_limit_bytes=VMEM_LIMIT),
        name="tail",
    )(x2d, ca, yn, proj2d, w_pw, b_pw, w_ssm, w_out, g_ffn, w_up, w_down, g_final)


def kernel(x_prompt, x_sample, state_conf_conv, state_ssm_conv, state_ssm, g_mix, w_in, conf_dw_w,
           conf_dw_b, conf_ln_g, conf_ln_b, conf_w_pw, conf_b_pw, ssm_conv_w, ssm_conv_b, ssm_dt_bias,
           ssm_a_log, ssm_d, ssm_norm_g, ssm_w_out, w_out, g_ffn, w_up, w_down, g_final):
    depth = w_in.shape[0]
    assert depth == 1
    nb, seq, _ = x_prompt.shape
    nreq, steps, _ = x_sample.shape
    i = 0

    def row(v):
        return v.reshape(1, -1)

    def pad_lanes(v, width):
        return jnp.pad(v, ((0, 0), (0, width - v.shape[1])))

    w_main = w_in[i][:, :MAIN_COLS].astype(BF16)
    w_dt = pad_lanes(w_in[i][:, MAIN_COLS:], DT_PAD).astype(BF16)
    b_dt = pad_lanes(row(ssm_dt_bias[i]), DT_PAD)
    a_log = pad_lanes(row(ssm_a_log[i]), DT_PAD)
    d_exp = row(jnp.repeat(ssm_d[i], HEADDIM))
    norm_g = row(ssm_norm_g[i])
    dw_w = conf_dw_w[i]
    dw_w32 = jnp.pad(dw_w, ((0, 32 - CONF_KERNEL), (0, 0)))
    dw_b, ln_g, ln_b = row(conf_dw_b[i]), row(conf_ln_g[i]), row(conf_ln_b[i])
    cw, cb = ssm_conv_w[i], row(ssm_conv_b[i])
    tail_w = (conf_w_pw[i].astype(BF16), row(conf_b_pw[i]), ssm_w_out[i].astype(BF16),
              w_out[i].astype(BF16), row(g_ffn[i]), w_up[i].astype(BF16), w_down[i].astype(BF16),
              row(g_final))

    xp = x_prompt.reshape(nb * seq, D_MODEL)
    proj_p, dt_p = _inproj(xp, row(g_mix[i]), w_main, w_dt, b_dt, tm=1024)
    proj_p3 = proj_p.reshape(nb, seq, MAIN_COLS)
    ca_p, conf_p = _conva_prompt(proj_p3, dw_w, dw_b, ln_g, ln_b)
    yn_p, scv_p, h_p = _ssd_prompt(proj_p3, dt_p.reshape(nb, seq, DT_PAD), cw, cb, a_log, d_exp, norm_g)
    y_p = _tail(xp, ca_p.reshape(nb * seq, C_CONV), yn_p.reshape(nb * seq, D_INNER), proj_p, *tail_w, tm=512)

    xs = x_sample.reshape(nreq * steps, D_MODEL)
    proj_s, dt_s = _inproj(xs, row(g_mix[i]), w_main, w_dt, b_dt, tm=nreq * steps)
    proj_s3 = proj_s.reshape(nreq, steps, MAIN_COLS)
    ca_s, conf_s = _conva_sample(proj_s3, state_conf_conv[i], dw_w32, dw_b, ln_g, ln_b)
    yn_s, scv_s, h_s = _ssd_sample(proj_s3, dt_s.reshape(nreq, steps, DT_PAD), state_ssm_conv[i],
                                   state_ssm[i], cw, cb, a_log, d_exp, norm_g)
    y_s = _tail(xs, ca_s, yn_s.reshape(nreq * steps, D_INNER), proj_s, *tail_w, tm=nreq * steps)

    return (y_p.reshape(nb, seq, D_MODEL), y_s.reshape(nreq, steps, D_MODEL),
            conf_p[None], scv_p[None], h_p[None], conf_s[None], scv_s[None], h_s[None])
```

```python
import jax
import jax.numpy as jnp
from jax import lax
from jax.experimental import pallas as pl
from jax.experimental.pallas import tpu as pltpu

F32 = jnp.float32
BF16 = jnp.bfloat16

D_MODEL = 1024
C_CONV = 1024
CONF_KERNEL = 31
CONF_HALO = CONF_KERNEL - 1
D_INNER = 2048
HEADDIM = 64
N_HEADS = 32
N_GROUPS = 8
HEADS_PER_GROUP = 4
GROUP_W = HEADS_PER_GROUP * HEADDIM
D_STATE = 128
SSM_CONV = 4
SSM_HALO = SSM_CONV - 1
CONV_DIM = 4096
CHUNK = 128
D_FF = 4096
EPS = 1e-6
MAIN_COLS = 10240
DT_PAD = 128
LANES = 128
CONF_SLABS = C_CONV // LANES
SSM_SLABS = CONV_DIM // LANES
X_SLABS = D_INNER // LANES
VMEM_LIMIT = 56 * 1024 * 1024


def _sigmoid(x):
    return jax.nn.sigmoid(x)


def _silu(x):
    return x * jax.nn.sigmoid(x)


def _softplus(x):
    return jnp.maximum(x, 0.0) + jnp.log1p(jnp.exp(-jnp.abs(x)))


def _rmsnorm(x, g):
    return x * lax.rsqrt(jnp.mean(x * x, axis=-1, keepdims=True) + EPS) * g


def _dot(a, b):
    return jnp.dot(a, b, preferred_element_type=F32)


def _dot_nt(a, b):
    return lax.dot_general(a, b, (((1,), (1,)), ((), ())), preferred_element_type=F32)


def _dot_tn(a, b):
    return lax.dot_general(a, b, (((0,), (0,)), ((), ())), preferred_element_type=F32)


def _split3(x):
    hi = x.astype(BF16)
    r1 = x - hi.astype(F32)
    mid = r1.astype(BF16)
    lo = (r1 - mid.astype(F32)).astype(BF16)
    return hi, mid, lo


def _cumsum_rows(a, tri):
    hi, mid, lo = _split3(a)
    return _dot(tri, hi) + _dot(tri, mid) + _dot(tri, lo)


def _lanes(j):
    return slice(j * LANES, (j + 1) * LANES)


def _const_spec(shape, index=None):
    nd = len(shape)
    idx = index if index is not None else (0,) * nd
    return pl.BlockSpec(shape, lambda *_: idx, pipeline_mode=pl.Buffered(1))


def _params(ngrid):
    return pltpu.CompilerParams(dimension_semantics=("arbitrary",) * ngrid, vmem_limit_bytes=VMEM_LIMIT)


def _ln_swish(y, g, b):
    mu = jnp.mean(y, axis=-1, keepdims=True)
    yc = y - mu
    yn = yc * lax.rsqrt(jnp.mean(yc * yc, axis=-1, keepdims=True) + EPS) * g + b
    return _silu(yn)


def _mlp_final(h, gffn_ref, wup_ref, wdown_ref, gfin_ref):
    hidden = jnp.square(jnp.maximum(_dot(_rmsnorm(h, gffn_ref[...]).astype(BF16), wup_ref[...]), 0.0))
    h = h + _dot(hidden.astype(BF16), wdown_ref[...])
    return _rmsnorm(h, gfin_ref[...])


CONVA_T = 512
CONVA_RB = 64
CONVA_PAD = 32


def _conva_prompt_kernel(x_ref, gmix_ref, wa_ref, wb_ref, wga_ref, dw_ref, dwb_ref, lng_ref, lnb_ref,
                         wpw_ref, bpw_ref, ga_ref, st_ref, xe, conv, u_scr):
    t = CONVA_T

    @pl.when(pl.program_id(1) == 0)
    def _():
        xe[:, 0:CONVA_PAD, :] = jnp.zeros((CONF_SLABS, CONVA_PAD, LANES), F32)

    u_scr[...] = _rmsnorm(x_ref[0], gmix_ref[...]).astype(BF16)
    glu = _dot(u_scr[...], wa_ref[...]) * _sigmoid(_dot(u_scr[...], wb_ref[...]))
    for j in range(CONF_SLABS):
        xe[j, CONVA_PAD:CONVA_PAD + t, :] = glu[:, _lanes(j)]
    off = CONVA_PAD - CONF_HALO
    for j in range(CONF_SLABS):
        for rb in range(t // CONVA_RB):
            r0 = rb * CONVA_RB
            acc = jnp.broadcast_to(dwb_ref[:, _lanes(j)], (CONVA_RB, LANES))
            for k in range(CONF_KERNEL):
                acc = acc + dw_ref[k:k + 1, _lanes(j)] * xe[j, r0 + off + k:r0 + off + k + CONVA_RB, :]
            conv[r0:r0 + CONVA_RB, _lanes(j)] = acc
    for j in range(CONF_SLABS):
        st_ref[0, :, _lanes(j)] = xe[j, t + off:t + CONVA_PAD, :]
        xe[j, 0:CONVA_PAD, :] = xe[j, t:t + CONVA_PAD, :]
    ca = _ln_swish(conv[...], lng_ref[...], lnb_ref[...]).astype(BF16)
    ga_ref[0] = _sigmoid(_dot(u_scr[...], wga_ref[...])) * (_dot(ca, wpw_ref[...]) + bpw_ref[...])


def _conva_prompt(x3, g_mix, w_main, dw_w, dw_b, ln_g, ln_b, w_pw, b_pw):
    nb, seq, _ = x3.shape
    t = CONVA_T
    vec = _const_spec((1, C_CONV))
    sq = (D_MODEL, D_MODEL)
    return pl.pallas_call(
        _conva_prompt_kernel,
        grid=(nb, seq // t),
        in_specs=[
            pl.BlockSpec((1, t, D_MODEL), lambda b, i: (b, i, 0)),
            vec,
            _const_spec(sq, (0, 0)), _const_spec(sq, (0, 1)), _const_spec(sq, (0, 2)),
            _const_spec((CONF_KERNEL, C_CONV)), vec, vec, vec,
            _const_spec(sq), vec,
        ],
        out_specs=[
            pl.BlockSpec((1, t, D_MODEL), lambda b, i: (b, i, 0)),
            pl.BlockSpec((1, CONF_HALO, C_CONV), lambda b, i: (b, 0, 0)),
        ],
        out_shape=[
            jax.ShapeDtypeStruct((nb, seq, D_MODEL), F32),
            jax.ShapeDtypeStruct((nb, CONF_HALO, C_CONV), F32),
        ],
        scratch_shapes=[
            pltpu.VMEM((CONF_SLABS, t + CONVA_PAD, LANES), F32),
            pltpu.VMEM((t, C_CONV), F32),
            pltpu.VMEM((t, D_MODEL), BF16),
        ],
        compiler_params=_params(2),
        name="conva_prompt",
    )(x3, g_mix, w_main, w_main, w_main, dw_w, dw_b, ln_g, ln_b, w_pw, b_pw)


def _expand_heads(v, g, lo_half):
    rows = v.shape[0]

    def col(e):
        return jnp.broadcast_to(v[:, 4 * g + e:4 * g + e + 1], (rows, LANES))

    return jnp.concatenate([jnp.where(lo_half, col(0), col(1)), jnp.where(lo_half, col(2), col(3))], axis=1)


def _gated_group_norm(y, z, g):
    y = y * _silu(z)
    return y * lax.rsqrt(jnp.mean(y * y, axis=-1, keepdims=True) + EPS) * g


def _ssd_chunk(rows, dt_scr, xs_scr, bc_scr, z_scr, yn_scr, h_ref, a_neg, dexp_ref, ng_ref):
    t = CHUNK
    r_i = lax.broadcasted_iota(jnp.int32, (t, t), 0)
    c_i = lax.broadcasted_iota(jnp.int32, (t, t), 1)
    causal = r_i >= c_i
    tri = causal.astype(BF16)
    lo_half = lax.broadcasted_iota(jnp.int32, (t, LANES), 1) < HEADDIM
    zeros = jnp.zeros((t, LANES), F32)

    dt = dt_scr[rows, :]
    acum = _cumsum_rows(dt * a_neg, tri)
    acum_t = acum.T
    a_last = acum[t - 1:t, :]
    eac = jnp.exp(acum)
    wsd = jnp.exp(a_last - acum)
    ealast = jnp.exp(a_last)

    for g in range(N_GROUPS):
        gs = slice(g * GROUP_W, (g + 1) * GROUP_W)
        xs = xs_scr[rows, gs]
        bm = bc_scr[rows, _lanes(g)]
        cm = bc_scr[rows, _lanes(N_GROUPS + g)]
        scores = _dot_nt(cm, bm)
        h_g = h_ref[0, 4 * g:4 * g + 4].reshape(GROUP_W, D_STATE)
        xdt = xs * _expand_heads(dt, g, lo_half)
        y = _dot_nt(cm, h_g.astype(BF16)) * _expand_heads(eac, g, lo_half) + xs * dexp_ref[:, gs]
        ms, xblk = [], []
        for e in range(HEADS_PER_GROUP):
            hd = 4 * g + e
            seg = acum[:, hd:hd + 1] - acum_t[hd:hd + 1, :]
            decay = jnp.exp(jnp.where(causal, seg, -jnp.inf))
            ms.append((scores * decay).astype(BF16))
            half = xdt[:, _lanes(e // 2)]
            keep = jnp.where(lo_half, half, 0.0) if e % 2 == 0 else jnp.where(lo_half, 0.0, half)
            blk = jnp.concatenate([keep, zeros] if e < 2 else [zeros, keep], axis=1)
            xblk.append(blk.astype(BF16))
        y = y + _dot(jnp.concatenate(ms, axis=1), jnp.concatenate(xblk, axis=0))
        xw = (xdt * _expand_heads(wsd, g, lo_half)).astype(BF16)
        upd = _dot_tn(xw, bm)
        for e in range(HEADS_PER_GROUP):
            hd = 4 * g + e
            h_ref[0, hd] = h_ref[0, hd] * ealast[:, hd:hd + 1] + upd[e * HEADDIM:(e + 1) * HEADDIM, :]
        yn_scr[rows, gs] = _gated_group_norm(y, z_scr[rows, gs], ng_ref[:, gs]).astype(BF16)


SSDP_T = 256
SSDP_RB = 64
SSD_PAD = 8


def _ssd_prompt_kernel(x_ref, gmix_ref, wz_ref, wx_ref, wbc_ref, wdt_ref, bdt_ref, wgb_ref, wssm_ref,
                       cw_ref, cb_ref, alog_ref, dexp_ref, ng_ref,
                       gb_ref, ncv_ref, h_ref,
                       xe, u_scr, xs_scr, bc_scr, z_scr, dt_scr, yn_scr):
    t = SSDP_T

    @pl.when(pl.program_id(1) == 0)
    def _():
        xe[:, 0:SSD_PAD, :] = jnp.zeros((SSM_SLABS, SSD_PAD, LANES), F32)
        h_ref[...] = jnp.zeros(h_ref.shape, F32)

    u_scr[...] = _rmsnorm(x_ref[0], gmix_ref[...]).astype(BF16)
    z_scr[...] = _dot(u_scr[...], wz_ref[...])
    dt_scr[...] = _softplus(_dot(u_scr[...], wdt_ref[...]) + bdt_ref[...])
    xr = _dot(u_scr[...], wx_ref[...])
    for j in range(X_SLABS):
        xe[j, SSD_PAD:SSD_PAD + t, :] = xr[:, _lanes(j)]
    bcr = _dot(u_scr[...], wbc_ref[...])
    for j in range(X_SLABS):
        xe[X_SLABS + j, SSD_PAD:SSD_PAD + t, :] = bcr[:, _lanes(j)]

    off = SSD_PAD - SSM_HALO
    for j in range(SSM_SLABS):
        ncv_ref[0, :, _lanes(j)] = xe[j, t + off:t + SSD_PAD, :]
        for rb in range(t // SSDP_RB):
            r0 = rb * SSDP_RB
            acc = jnp.broadcast_to(cb_ref[:, _lanes(j)], (SSDP_RB, LANES))
            for k in range(SSM_CONV):
                acc = acc + cw_ref[k:k + 1, _lanes(j)] * xe[j, r0 + off + k:r0 + off + k + SSDP_RB, :]
            if j < X_SLABS:
                xs_scr[r0:r0 + SSDP_RB, _lanes(j)] = _silu(acc)
            else:
                bc_scr[r0:r0 + SSDP_RB, _lanes(j - X_SLABS)] = _silu(acc).astype(BF16)
        xe[j, 0:SSD_PAD, :] = xe[j, t:t + SSD_PAD, :]

    a_neg = -jnp.exp(alog_ref[...])

    def chunk(c, carry):
        rows = pl.ds(pl.multiple_of(c * CHUNK, CHUNK), CHUNK)
        _ssd_chunk(rows, dt_scr, xs_scr, bc_scr, z_scr, yn_scr, h_ref, a_neg, dexp_ref, ng_ref)
        return carry

    lax.fori_loop(0, t // CHUNK, chunk, 0)
    gb_ref[0] = _sigmoid(_dot(u_scr[...], wgb_ref[...])) * _dot(yn_scr[...], wssm_ref[...])


def _ssd_prompt(x3, g_mix, w_main, w_dt, b_dt, w_ssm, conv_w, conv_b, a_log, d_exp, norm_g):
    nb, seq, _ = x3.shape
    t = SSDP_T
    wide = (D_MODEL, D_INNER)
    return pl.pallas_call(
        _ssd_prompt_kernel,
        grid=(nb, seq // t),
        in_specs=[
            pl.BlockSpec((1, t, D_MODEL), lambda b, i: (b, i, 0)),
            _const_spec((1, D_MODEL)),
            _const_spec(wide, (0, 2)), _const_spec(wide, (0, 3)), _const_spec(wide, (0, 4)),
            _const_spec((D_MODEL, DT_PAD)), _const_spec((1, DT_PAD)),
            _const_spec((D_MODEL, D_MODEL), (0, 3)),
            _const_spec((D_INNER, D_MODEL)),
            _const_spec((SSM_CONV, CONV_DIM)), _const_spec((1, CONV_DIM)), _const_spec((1, DT_PAD)),
            _const_spec((1, D_INNER)), _const_spec((1, D_INNER)),
        ],
        out_specs=[
            pl.BlockSpec((1, t, D_MODEL), lambda b, i: (b, i, 0)),
            pl.BlockSpec((1, SSM_HALO, CONV_DIM), lambda b, i: (b, 0, 0)),
            pl.BlockSpec((1, N_HEADS, HEADDIM, D_STATE), lambda b, i: (b, 0, 0, 0)),
        ],
        out_shape=[
            jax.ShapeDtypeStruct((nb, seq, D_MODEL), F32),
            jax.ShapeDtypeStruct((nb, SSM_HALO, CONV_DIM), F32),
            jax.ShapeDtypeStruct((nb, N_HEADS, HEADDIM, D_STATE), F32),
        ],
        scratch_shapes=[
            pltpu.VMEM((SSM_SLABS, t + SSD_PAD, LANES), F32),
            pltpu.VMEM((t, D_MODEL), BF16),
            pltpu.VMEM((t, D_INNER), F32),
            pltpu.VMEM((t, D_INNER), BF16),
            pltpu.VMEM((t, D_INNER), F32),
            pltpu.VMEM((t, DT_PAD), F32),
            pltpu.VMEM((t, D_INNER), BF16),
        ],
        compiler_params=_params(2),
        name="ssd_prompt",
    )(x3, g_mix, w_main, w_main, w_main, w_dt, b_dt, w_main, w_ssm, conv_w, conv_b, a_log, d_exp, norm_g)


def _tail_prompt_kernel(x_ref, ga_ref, gb_ref, wout_ref, gffn_ref, wup_ref, wdown_ref, gfin_ref, y_ref):
    merged = (ga_ref[...] + gb_ref[...]).astype(BF16)
    h = x_ref[...] + _dot(merged, wout_ref[...])
    y_ref[...] = _mlp_final(h, gffn_ref, wup_ref, wdown_ref, gfin_ref)


def _tail_prompt(x2d, ga, gb, w_out, g_ffn, w_up, w_down, g_final, tm=512):
    m = x2d.shape[0]
    tile = pl.BlockSpec((tm, D_MODEL), lambda i: (i, 0))
    vec = _const_spec((1, D_MODEL))
    return pl.pallas_call(
        _tail_prompt_kernel,
        grid=(m // tm,),
        in_specs=[tile, tile, tile, _const_spec((D_MODEL, D_MODEL)), vec,
                  _const_spec((D_MODEL, D_FF)), _const_spec((D_FF, D_MODEL)), vec],
        out_specs=tile,
        out_shape=jax.ShapeDtypeStruct((m, D_MODEL), F32),
        compiler_params=_params(1),
        name="tail_prompt",
    )(x2d, ga, gb, w_out, g_ffn, w_up, w_down, g_final)


def _inproj_kernel(x_ref, g_ref, w_ref, wdt_ref, bdt_ref, proj_ref, dt_ref, u_scr):
    @pl.when(pl.program_id(0) == 0)
    def _():
        ub = _rmsnorm(x_ref[...], g_ref[...]).astype(BF16)
        u_scr[...] = ub
        dt_ref[...] = _softplus(_dot(ub, wdt_ref[...]) + bdt_ref[...])

    proj_ref[...] = _dot(u_scr[...], w_ref[...])


def _inproj(x2d, g_mix, w_main, w_dt, b_dt, tn=1024):
    m = x2d.shape[0]
    return pl.pallas_call(
        _inproj_kernel,
        grid=(MAIN_COLS // tn,),
        in_specs=[
            pl.BlockSpec((m, D_MODEL), lambda j: (0, 0)),
            pl.BlockSpec((1, D_MODEL), lambda j: (0, 0)),
            pl.BlockSpec((D_MODEL, tn), lambda j: (0, j)),
            pl.BlockSpec((D_MODEL, DT_PAD), lambda j: (0, 0)),
            pl.BlockSpec((1, DT_PAD), lambda j: (0, 0)),
        ],
        out_specs=[
            pl.BlockSpec((m, tn), lambda j: (0, j)),
            pl.BlockSpec((m, DT_PAD), lambda j: (0, 0)),
        ],
        out_shape=[
            jax.ShapeDtypeStruct((m, MAIN_COLS), F32),
            jax.ShapeDtypeStruct((m, DT_PAD), F32),
        ],
        scratch_shapes=[pltpu.VMEM((m, D_MODEL), BF16)],
        compiler_params=_params(1),
        name="inproj_sample",
    )(x2d, g_mix, w_main, w_dt, b_dt)


CONVA_SR = 32
CONVA_SROWS = 40


def _conva_sample_kernel(a_ref, b_ref, st_ref, w_ref, bias_ref, lng_ref, lnb_ref,
                         ca_ref, nst_ref, xe, conv):
    nreq, steps = a_ref.shape[0], a_ref.shape[1]
    xe[:, 32:CONVA_SROWS, :] = jnp.zeros((CONF_SLABS, CONVA_SROWS - 32, LANES), F32)
    per = 8 // steps

    def body(p, carry):
        rows = []
        for q in range(per):
            r = p * per + q
            st = st_ref[r]
            glu = a_ref[r] * _sigmoid(b_ref[r])
            for j in range(CONF_SLABS):
                xe[j, 0:CONF_HALO, :] = st[:, _lanes(j)]
                xe[j, CONF_HALO:CONF_HALO + steps, :] = glu[:, _lanes(j)]
                nst_ref[r, :, _lanes(j)] = xe[j, steps:steps + CONF_HALO, :]
            for t in range(steps):
                rows.append(jnp.concatenate(
                    [jnp.sum(w_ref[:, _lanes(j)] * xe[j, t:t + 32, :], axis=0, keepdims=True)
                     for j in range(CONF_SLABS)], axis=1))
        conv[pl.ds(pl.multiple_of(p * 8, 8), 8), :] = jnp.concatenate(rows, axis=0) + bias_ref[...]
        return carry

    lax.fori_loop(0, nreq // per, body, 0)
    ca_ref[...] = _ln_swish(conv[...], lng_ref[...], lnb_ref[...]).astype(BF16)


def _conva_sample(proj3, state, dw_w32, dw_b, ln_g, ln_b):
    nreq, steps, _ = proj3.shape
    r = CONVA_SR
    vec = pl.BlockSpec((1, C_CONV), lambda i: (0, 0))
    return pl.pallas_call(
        _conva_sample_kernel,
        grid=(nreq // r,),
        in_specs=[
            pl.BlockSpec((r, steps, C_CONV), lambda i: (i, 0, 0)),
            pl.BlockSpec((r, steps, C_CONV), lambda i: (i, 0, 1)),
            pl.BlockSpec((r, CONF_HALO, C_CONV), lambda i: (i, 0, 0)),
            pl.BlockSpec((32, C_CONV), lambda i: (0, 0)),
            vec, vec, vec,
        ],
        out_specs=[
            pl.BlockSpec((r * steps, C_CONV), lambda i: (i, 0)),
            pl.BlockSpec((r, CONF_HALO, C_CONV), lambda i: (i, 0, 0)),
        ],
        out_shape=[
            jax.ShapeDtypeStruct((nreq * steps, C_CONV), BF16),
            jax.ShapeDtypeStruct((nreq, CONF_HALO, C_CONV), F32),
        ],
        scratch_shapes=[pltpu.VMEM((CONF_SLABS, CONVA_SROWS, LANES), F32),
                        pltpu.VMEM((r * steps, C_CONV), F32)],
        compiler_params=_params(1),
        name="conva_sample",
    )(proj3, proj3, state, dw_w32, dw_b, ln_g, ln_b)


SSD_SR = 4
SSD_ST = 8
SSD_SROWS = 16


def _ssd_sample_kernel(z_ref, xr_ref, bcr_ref, dt_ref, cst_ref, h0_ref, cw_ref, cb_ref, alog_ref,
                       dexp_ref, ng_ref, yn_ref, ncv_ref, h_ref, xe, dt8, z8):
    nreq, steps = z_ref.shape[0], z_ref.shape[1]
    t = SSD_ST
    xe[...] = jnp.zeros(xe.shape, F32)
    dt8[...] = jnp.zeros(dt8.shape, F32)
    z8[...] = jnp.zeros(z8.shape, F32)
    rows = lax.broadcasted_iota(jnp.int32, (t, 1), 0)
    live = rows < steps
    lo_half = lax.broadcasted_iota(jnp.int32, (t, LANES), 1) < HEADDIM
    a_neg = -jnp.exp(alog_ref[...])

    def body(r, carry):
        cst = cst_ref[r]
        xr = xr_ref[r]
        bcr = bcr_ref[r]
        for j in range(SSM_SLABS):
            xe[j, 0:SSM_HALO, :] = cst[:, _lanes(j)]
            new = xr[:, _lanes(j)] if j < X_SLABS else bcr[:, _lanes(j - X_SLABS)]
            xe[j, SSM_HALO:SSM_HALO + steps, :] = new
            ncv_ref[r, :, _lanes(j)] = xe[j, steps:steps + SSM_HALO, :]
        dt8[0:steps, :] = dt_ref[r]
        z8[0:steps, :] = z_ref[r]

        def conv(first, count):
            tiles = []
            for j in range(first, first + count):
                acc = jnp.broadcast_to(cb_ref[:, _lanes(j)], (t, LANES))
                for k in range(SSM_CONV):
                    acc = acc + cw_ref[k:k + 1, _lanes(j)] * xe[j, k:k + t, :]
                tiles.append(jnp.where(live, _silu(acc), 0.0))
            return tiles[0] if count == 1 else jnp.concatenate(tiles, axis=1)

        dt = dt8[...]
        a = dt * a_neg
        acum = jnp.zeros((t, DT_PAD), F32)
        for s in range(steps):
            acum = acum + jnp.where(rows >= s, a[s:s + 1, :], 0.0)
        a_last = acum[t - 1:t, :]
        eac = jnp.exp(acum)
        wsd = jnp.exp(a_last - acum)
        ealast = jnp.exp(a_last)

        for g in range(N_GROUPS):
            gs = slice(g * GROUP_W, (g + 1) * GROUP_W)
            xs = conv(2 * g, 2)
            bm = conv(X_SLABS + g, 1)
            cm = conv(X_SLABS + N_GROUPS + g, 1)
            scores = _dot_nt(cm, bm)
            h_g = h0_ref[r, 4 * g:4 * g + 4].reshape(GROUP_W, D_STATE)
            xdt = xs * _expand_heads(dt, g, lo_half)
            acx = _expand_heads(acum, g, lo_half)
            y = _dot_nt(cm, h_g) * _expand_heads(eac, g, lo_half) + xs * dexp_ref[:, gs]
            for s in range(steps):
                decay = jnp.exp(jnp.where(rows >= s, acx - acx[s:s + 1, :], -jnp.inf))
                y = y + (scores[:, s:s + 1] * decay) * xdt[s:s + 1, :]
            upd = _dot_tn(xdt * _expand_heads(wsd, g, lo_half), bm)
            for e in range(HEADS_PER_GROUP):
                hd = 4 * g + e
                h_ref[r, hd] = h0_ref[r, hd] * ealast[:, hd:hd + 1] + upd[e * HEADDIM:(e + 1) * HEADDIM, :]
            yn = _gated_group_norm(y, z8[:, gs], ng_ref[:, gs])
            yn_ref[r, :, gs] = yn[0:steps, :]
        return carry

    lax.fori_loop(0, nreq, body, 0)


def _ssd_sample(proj3, dt3, conv_state, h0, conv_w, conv_b, a_log, d_exp, norm_g):
    nreq, steps, _ = proj3.shape
    r = SSD_SR
    return pl.pallas_call(
        _ssd_sample_kernel,
        grid=(nreq // r,),
        in_specs=[
            pl.BlockSpec((r, steps, D_INNER), lambda i: (i, 0, 2)),
            pl.BlockSpec((r, steps, D_INNER), lambda i: (i, 0, 3)),
            pl.BlockSpec((r, steps, D_INNER), lambda i: (i, 0, 4)),
            pl.BlockSpec((r, steps, DT_PAD), lambda i: (i, 0, 0)),
            pl.BlockSpec((r, SSM_HALO, CONV_DIM), lambda i: (i, 0, 0)),
            pl.BlockSpec((r, N_HEADS, HEADDIM, D_STATE), lambda i: (i, 0, 0, 0)),
            pl.BlockSpec((SSM_CONV, CONV_DIM), lambda i: (0, 0)),
            pl.BlockSpec((1, CONV_DIM), lambda i: (0, 0)),
            pl.BlockSpec((1, DT_PAD), lambda i: (0, 0)),
            pl.BlockSpec((1, D_INNER), lambda i: (0, 0)),
            pl.BlockSpec((1, D_INNER), lambda i: (0, 0)),
        ],
        out_specs=[
            pl.BlockSpec((r, steps, D_INNER), lambda i: (i, 0, 0)),
            pl.BlockSpec((r, SSM_HALO, CONV_DIM), lambda i: (i, 0, 0)),
            pl.BlockSpec((r, N_HEADS, HEADDIM, D_STATE), lambda i: (i, 0, 0, 0)),
        ],
        out_shape=[
            jax.ShapeDtypeStruct((nreq, steps, D_INNER), F32),
            jax.ShapeDtypeStruct((nreq, SSM_HALO, CONV_DIM), F32),
            jax.ShapeDtypeStruct((nreq, N_HEADS, HEADDIM, D_STATE), F32),
        ],
        scratch_shapes=[
            pltpu.VMEM((SSM_SLABS, SSD_SROWS, LANES), F32),
            pltpu.VMEM((SSD_ST, DT_PAD), F32),
            pltpu.VMEM((SSD_ST, D_INNER), F32),
        ],
        compiler_params=_params(1),
        name="ssd_sample",
    )(proj3, proj3, proj3, dt3, conv_state, h0, conv_w, conv_b, a_log, d_exp, norm_g)


def _tail_sample_kernel(x_ref, ca_ref, yn_ref, gate_ref, wpw_ref, bpw_ref, wssm_ref, wout_ref, gffn_ref,
                        wup_ref, wdown_ref, gfin_ref, y_ref):
    branch_a = _dot(ca_ref[...], wpw_ref[...]) + bpw_ref[...]
    branch_b = _dot(yn_ref[...].astype(BF16), wssm_ref[...])
    merged = (_sigmoid(gate_ref[:, 0:D_MODEL]) * branch_a
              + _sigmoid(gate_ref[:, D_MODEL:2 * D_MODEL]) * branch_b)
    h = x_ref[...] + _dot(merged.astype(BF16), wout_ref[...])
    y_ref[...] = _mlp_final(h, gffn_ref, wup_ref, wdown_ref, gfin_ref)


def _tail_sample(x2d, ca, yn, proj2d, w_pw, b_pw, w_ssm, w_out, g_ffn, w_up, w_down, g_final):
    m = x2d.shape[0]
    vec = _const_spec((1, D_MODEL))
    return pl.pallas_call(
        _tail_sample_kernel,
        grid=(1,),
        in_specs=[
            pl.BlockSpec((m, D_MODEL), lambda i: (0, 0)),
            pl.BlockSpec((m, C_CONV), lambda i: (0, 0)),
            pl.BlockSpec((m, D_INNER), lambda i: (0, 0)),
            pl.BlockSpec((m, 2 * D_MODEL), lambda i: (0, 1)),
            _const_spec((C_CONV, D_MODEL)), vec, _const_spec((D_INNER, D_MODEL)),
            _const_spec((D_MODEL, D_MODEL)), vec, _const_spec((D_MODEL, D_FF)),
            _const_spec((D_FF, D_MODEL)), vec,
        ],
        out_specs=pl.BlockSpec((m, D_MODEL), lambda i: (0, 0)),
        out_shape=jax.ShapeDtypeStruct((m, D_MODEL), F32),
        compiler_params=_params(1),
        name="tail_sample",
    )(x2d, ca, yn, proj2d, w_pw, b_pw, w_ssm, w_out, g_ffn, w_up, w_down, g_final)


def kernel(x_prompt, x_sample, state_conf_conv, state_ssm_conv, state_ssm, g_mix, w_in, conf_dw_w,
           conf_dw_b, conf_ln_g, conf_ln_b, conf_w_pw, conf_b_pw, ssm_conv_w, ssm_conv_b, ssm_dt_bias,
           ssm_a_log, ssm_d, ssm_norm_g, ssm_w_out, w_out, g_ffn, w_up, w_down, g_final):
    depth = w_in.shape[0]
    assert depth == 1
    nb, seq, _ = x_prompt.shape
    nreq, steps, _ = x_sample.shape
    i = 0

    def row(v):
        return v.reshape(1, -1)

    def pad_lanes(v, width):
        return jnp.pad(v, ((0, 0), (0, width - v.shape[1])))

    gm = row(g_mix[i])
    w_main = w_in[i][:, :MAIN_COLS].astype(BF16)
    w_dt = pad_lanes(w_in[i][:, MAIN_COLS:], DT_PAD).astype(BF16)
    b_dt = pad_lanes(row(ssm_dt_bias[i]), DT_PAD)
    a_log = pad_lanes(row(ssm_a_log[i]), DT_PAD)
    d_exp = row(jnp.repeat(ssm_d[i], HEADDIM))
    norm_g = row(ssm_norm_g[i])
    dw_w = conf_dw_w[i]
    dw_w32 = jnp.pad(dw_w, ((0, 32 - CONF_KERNEL), (0, 0)))
    dw_b, ln_g, ln_b = row(conf_dw_b[i]), row(conf_ln_g[i]), row(conf_ln_b[i])
    cw, cb = ssm_conv_w[i], row(ssm_conv_b[i])
    w_pw, b_pw = conf_w_pw[i].astype(BF16), row(conf_b_pw[i])
    w_ssm = ssm_w_out[i].astype(BF16)
    mlp_w = (w_out[i].astype(BF16), row(g_ffn[i]), w_up[i].astype(BF16), w_down[i].astype(BF16),
             row(g_final))

    ga, conf_p = _conva_prompt(x_prompt, gm, w_main, dw_w, dw_b, ln_g, ln_b, w_pw, b_pw)
    gb, scv_p, h_p = _ssd_prompt(x_prompt, gm, w_main, w_dt, b_dt, w_ssm, cw, cb, a_log, d_exp, norm_g)
    m_p = nb * seq
    y_p = _tail_prompt(x_prompt.reshape(m_p, D_MODEL), ga.reshape(m_p, D_MODEL), gb.reshape(m_p, D_MODEL),
                       *mlp_w)

    xs = x_sample.reshape(nreq * steps, D_MODEL)
    proj_s, dt_s = _inproj(xs, gm, w_main, w_dt, b_dt)
    proj_s3 = proj_s.reshape(nreq, steps, MAIN_COLS)
    ca_s, conf_s = _conva_sample(proj_s3, state_conf_conv[i], dw_w32, dw_b, ln_g, ln_b)
    yn_s, scv_s, h_s = _ssd_sample(proj_s3, dt_s.reshape(nreq, steps, DT_PAD), state_ssm_conv[i],
                                   state_ssm[i], cw, cb, a_log, d_exp, norm_g)
    y_s = _tail_sample(xs, ca_s, yn_s.reshape(nreq * steps, D_INNER), proj_s, w_pw, b_pw, w_ssm, *mlp_w)

    return (y_p.reshape(nb, seq, D_MODEL), y_s.reshape(nreq, steps, D_MODEL),
            conf_p[None], scv_p[None], h_p[None], conf_s[None], scv_s[None], h_s[None])
```

```python
import jax
import jax.numpy as jnp
from jax import lax
from jax.experimental import pallas as pl
from jax.experimental.pallas import tpu as pltpu

F32 = jnp.float32
BF16 = jnp.bfloat16

D_MODEL = 1024
C_CONV = 1024
CONF_KERNEL = 31
CONF_HALO = CONF_KERNEL - 1
D_INNER = 2048
HEADDIM = 64
N_HEADS = 32
N_GROUPS = 8
HEADS_PER_GROUP = 4
GROUP_W = HEADS_PER_GROUP * HEADDIM
D_STATE = 128
SSM_CONV = 4
SSM_HALO = SSM_CONV - 1
CONV_DIM = 4096
CHUNK = 128
D_FF = 4096
EPS = 1e-6
MAIN_COLS = 10240
DT_PAD = 128
LANES = 128
CONF_SLABS = C_CONV // LANES
SSM_SLABS = CONV_DIM // LANES
X_SLABS = D_INNER // LANES
VMEM_LIMIT = 56 * 1024 * 1024


def _sigmoid(x):
    return jax.nn.sigmoid(x)


def _silu(x):
    return x * jax.nn.sigmoid(x)


def _softplus(x):
    return jnp.maximum(x, 0.0) + jnp.log1p(jnp.exp(-jnp.abs(x)))


def _rmsnorm(x, g):
    return x * lax.rsqrt(jnp.mean(x * x, axis=-1, keepdims=True) + EPS) * g


def _dot(a, b):
    return jnp.dot(a, b, preferred_element_type=F32)


def _dot_nt(a, b):
    return lax.dot_general(a, b, (((1,), (1,)), ((), ())), preferred_element_type=F32)


def _dot_tn(a, b):
    return lax.dot_general(a, b, (((0,), (0,)), ((), ())), preferred_element_type=F32)


def _split3(x):
    hi = x.astype(BF16)
    r1 = x - hi.astype(F32)
    mid = r1.astype(BF16)
    lo = (r1 - mid.astype(F32)).astype(BF16)
    return hi, mid, lo


def _cumsum_rows(a, tri):
    hi, mid, lo = _split3(a)
    return _dot(tri, hi) + _dot(tri, mid) + _dot(tri, lo)


def _lanes(j):
    return slice(j * LANES, (j + 1) * LANES)


def _const_spec(shape, index=None):
    nd = len(shape)
    idx = index if index is not None else (0,) * nd
    return pl.BlockSpec(shape, lambda *_: idx, pipeline_mode=pl.Buffered(1))


def _params(ngrid):
    return pltpu.CompilerParams(dimension_semantics=("arbitrary",) * ngrid, vmem_limit_bytes=VMEM_LIMIT)


def _ln_swish(y, g, b):
    mu = jnp.mean(y, axis=-1, keepdims=True)
    yc = y - mu
    yn = yc * lax.rsqrt(jnp.mean(yc * yc, axis=-1, keepdims=True) + EPS) * g + b
    return _silu(yn)


def _mlp_final(h, gffn_ref, wup_ref, wdown_ref, gfin_ref):
    hidden = jnp.square(jnp.maximum(_dot(_rmsnorm(h, gffn_ref[...]).astype(BF16), wup_ref[...]), 0.0))
    h = h + _dot(hidden.astype(BF16), wdown_ref[...])
    return _rmsnorm(h, gfin_ref[...])


CONVA_T = 512
CONVA_CH = 256
CONVA_RB = 64
CONVA_PAD = 32


def _conva_prompt_kernel(x_ref, gmix_ref, wa_ref, wb_ref, wga_ref, dw_ref, dwb_ref, lng_ref, lnb_ref,
                         wpw_ref, bpw_ref, ga_ref, st_ref, xe, conv, u_scr):
    t = CONVA_T

    @pl.when(pl.program_id(1) == 0)
    def _():
        xe[:, 0:CONVA_PAD, :] = jnp.zeros((CONF_SLABS, CONVA_PAD, LANES), F32)

    u_scr[...] = _rmsnorm(x_ref[0], gmix_ref[...]).astype(BF16)
    off = CONVA_PAD - CONF_HALO
    for c in range(t // CONVA_CH):
        c0 = c * CONVA_CH
        u_c = u_scr[c0:c0 + CONVA_CH, :]
        glu = _dot(u_c, wa_ref[...]) * _sigmoid(_dot(u_c, wb_ref[...]))
        for j in range(CONF_SLABS):
            xe[j, CONVA_PAD + c0:CONVA_PAD + c0 + CONVA_CH, :] = glu[:, _lanes(j)]
        for j in range(CONF_SLABS):
            for rb in range(CONVA_CH // CONVA_RB):
                r0 = c0 + rb * CONVA_RB
                acc = jnp.broadcast_to(dwb_ref[:, _lanes(j)], (CONVA_RB, LANES))
                for k in range(CONF_KERNEL):
                    acc = acc + dw_ref[k:k + 1, _lanes(j)] * xe[j, r0 + off + k:r0 + off + k + CONVA_RB, :]
                conv[r0:r0 + CONVA_RB, _lanes(j)] = acc
        ca = _ln_swish(conv[c0:c0 + CONVA_CH, :], lng_ref[...], lnb_ref[...]).astype(BF16)
        ga_ref[0, c0:c0 + CONVA_CH, :] = (_sigmoid(_dot(u_c, wga_ref[...]))
                                          * (_dot(ca, wpw_ref[...]) + bpw_ref[...]))
    for j in range(CONF_SLABS):
        st_ref[0, :, _lanes(j)] = xe[j, t + off:t + CONVA_PAD, :]
        xe[j, 0:CONVA_PAD, :] = xe[j, t:t + CONVA_PAD, :]


def _conva_prompt(x3, g_mix, w_main, dw_w, dw_b, ln_g, ln_b, w_pw, b_pw):
    nb, seq, _ = x3.shape
    t = CONVA_T
    vec = _const_spec((1, C_CONV))
    sq = (D_MODEL, D_MODEL)
    return pl.pallas_call(
        _conva_prompt_kernel,
        grid=(nb, seq // t),
        in_specs=[
            pl.BlockSpec((1, t, D_MODEL), lambda b, i: (b, i, 0)),
            vec,
            _const_spec(sq, (0, 0)), _const_spec(sq, (0, 1)), _const_spec(sq, (0, 2)),
            _const_spec((CONF_KERNEL, C_CONV)), vec, vec, vec,
            _const_spec(sq), vec,
        ],
        out_specs=[
            pl.BlockSpec((1, t, D_MODEL), lambda b, i: (b, i, 0)),
            pl.BlockSpec((1, CONF_HALO, C_CONV), lambda b, i: (b, 0, 0)),
        ],
        out_shape=[
            jax.ShapeDtypeStruct((nb, seq, D_MODEL), F32),
            jax.ShapeDtypeStruct((nb, CONF_HALO, C_CONV), F32),
        ],
        scratch_shapes=[
            pltpu.VMEM((CONF_SLABS, t + CONVA_PAD, LANES), F32),
            pltpu.VMEM((t, C_CONV), F32),
            pltpu.VMEM((t, D_MODEL), BF16),
        ],
        compiler_params=_params(2),
        name="conva_prompt",
    )(x3, g_mix, w_main, w_main, w_main, dw_w, dw_b, ln_g, ln_b, w_pw, b_pw)


def _expand_heads(v, g, lo_half):
    rows = v.shape[0]

    def col(e):
        return jnp.broadcast_to(v[:, 4 * g + e:4 * g + e + 1], (rows, LANES))

    return jnp.concatenate([jnp.where(lo_half, col(0), col(1)), jnp.where(lo_half, col(2), col(3))], axis=1)


def _gated_group_norm(y, z, g):
    y = y * _silu(z)
    return y * lax.rsqrt(jnp.mean(y * y, axis=-1, keepdims=True) + EPS) * g


LOG2E = 1.4426950408889634


def _cat3(v):
    return jnp.concatenate(_split3(v), axis=1)


def _ssd_chunk(rows, dt, xs_scr, bc_scr, z_scr, yn_scr, h_ref, a_neg, dexp_ref, ng_ref, ex_ref, acumt_scr):
    t = CHUNK
    r_i = lax.broadcasted_iota(jnp.int32, (t, t), 0)
    c_i = lax.broadcasted_iota(jnp.int32, (t, t), 1)
    causal = r_i >= c_i
    tri = causal.astype(BF16)
    lo_half = lax.broadcasted_iota(jnp.int32, (t, LANES), 1) < HEADDIM
    zeros = jnp.zeros((t, LANES), F32)
    zeros_b = jnp.zeros((t, t), BF16)

    acum = _cumsum_rows(dt * a_neg, tri)
    a_last = acum[t - 1:t, :]
    ealast = jnp.exp(a_last)
    acum2 = acum * LOG2E
    acumt_scr[...] = acum2.T
    dt3 = _cat3(dt)

    for g in range(N_GROUPS):
        gs = slice(g * GROUP_W, (g + 1) * GROUP_W)
        xs = xs_scr[rows, gs]
        bm = bc_scr[rows, _lanes(g)]
        cm = bc_scr[rows, _lanes(N_GROUPS + g)]
        scores = _dot_nt(cm, bm)
        h_g = h_ref[0, 4 * g:4 * g + 4].reshape(GROUP_W, D_STATE)
        colb = [jnp.broadcast_to(acum2[:, 4 * g + e:4 * g + e + 1], (t, LANES))
                for e in range(HEADS_PER_GROUP)]
        acx = jnp.concatenate([jnp.where(lo_half, colb[0], colb[1]),
                               jnp.where(lo_half, colb[2], colb[3])], axis=1)
        xdt = xs * _dot(dt3, ex_ref[:, gs])
        y = _dot_nt(cm, h_g.astype(BF16)) * jnp.exp2(acx) + xs * dexp_ref[:, gs]
        ms, xblk = [], []
        for e in range(HEADS_PER_GROUP):
            hd = 4 * g + e
            seg = colb[e] - acumt_scr[hd:hd + 1, :]
            m = (scores * jnp.exp2(seg)).astype(BF16)
            ms.append(jnp.where(causal, m, zeros_b))
            half = xdt[:, _lanes(e // 2)]
            keep = jnp.where(lo_half, half, 0.0) if e % 2 == 0 else jnp.where(lo_half, 0.0, half)
            blk = jnp.concatenate([keep, zeros] if e < 2 else [zeros, keep], axis=1)
            xblk.append(blk.astype(BF16))
        y = y + _dot(jnp.concatenate(ms, axis=1), jnp.concatenate(xblk, axis=0))
        xw = (xdt * jnp.exp2(acx[t - 1:t, :] - acx)).astype(BF16)
        upd = _dot_tn(xw, bm)
        for e in range(HEADS_PER_GROUP):
            hd = 4 * g + e
            h_ref[0, hd] = h_ref[0, hd] * ealast[:, hd:hd + 1] + upd[e * HEADDIM:(e + 1) * HEADDIM, :]
        yn_scr[rows, gs] = _gated_group_norm(y, z_scr[rows, gs], ng_ref[:, gs]).astype(BF16)


SSDP_T = 256
SSDP_RB = 64
SSD_PAD = 8


def _ssd_prompt_kernel(x_ref, gmix_ref, wz_ref, wx_ref, wbc_ref, wdt_ref, bdt_ref, wgb_ref, wssm_ref,
                       cw_ref, cb_ref, alog_ref, dexp_ref, ng_ref, ex_ref,
                       gb_ref, ncv_ref, h_ref,
                       xe, u_scr, xs_scr, bc_scr, z_scr, yn_scr, acumt_scr):
    t = SSDP_T

    @pl.when(pl.program_id(1) == 0)
    def _():
        xe[:, 0:SSD_PAD, :] = jnp.zeros((SSM_SLABS, SSD_PAD, LANES), F32)
        h_ref[...] = jnp.zeros(h_ref.shape, F32)

    u_scr[...] = _rmsnorm(x_ref[0], gmix_ref[...]).astype(BF16)
    a_neg = -jnp.exp(alog_ref[...])
    off = SSD_PAD - SSM_HALO
    for c in range(t // CHUNK):
        c0 = c * CHUNK
        rows = slice(c0, c0 + CHUNK)
        u_c = u_scr[rows, :]
        z_scr[rows, :] = _dot(u_c, wz_ref[...])
        dt = _softplus(_dot(u_c, wdt_ref[...]) + bdt_ref[...])
        xr = _dot(u_c, wx_ref[...])
        for j in range(X_SLABS):
            xe[j, SSD_PAD + c0:SSD_PAD + c0 + CHUNK, :] = xr[:, _lanes(j)]
        bcr = _dot(u_c, wbc_ref[...])
        for j in range(X_SLABS):
            xe[X_SLABS + j, SSD_PAD + c0:SSD_PAD + c0 + CHUNK, :] = bcr[:, _lanes(j)]
        for j in range(SSM_SLABS):
            for rb in range(CHUNK // SSDP_RB):
                r0 = c0 + rb * SSDP_RB
                acc = jnp.broadcast_to(cb_ref[:, _lanes(j)], (SSDP_RB, LANES))
                for k in range(SSM_CONV):
                    acc = acc + cw_ref[k:k + 1, _lanes(j)] * xe[j, r0 + off + k:r0 + off + k + SSDP_RB, :]
                if j < X_SLABS:
                    xs_scr[r0:r0 + SSDP_RB, _lanes(j)] = _silu(acc)
                else:
                    bc_scr[r0:r0 + SSDP_RB, _lanes(j - X_SLABS)] = _silu(acc).astype(BF16)
        _ssd_chunk(rows, dt, xs_scr, bc_scr, z_scr, yn_scr, h_ref, a_neg, dexp_ref, ng_ref, ex_ref, acumt_scr)
        gb_ref[0, rows, :] = _sigmoid(_dot(u_c, wgb_ref[...])) * _dot(yn_scr[rows, :], wssm_ref[...])
    for j in range(SSM_SLABS):
        ncv_ref[0, :, _lanes(j)] = xe[j, t + off:t + SSD_PAD, :]
        xe[j, 0:SSD_PAD, :] = xe[j, t:t + SSD_PAD, :]


def _ssd_prompt(x3, g_mix, w_main, w_dt, b_dt, w_ssm, conv_w, conv_b, a_log, d_exp, norm_g, ex):
    nb, seq, _ = x3.shape
    t = SSDP_T
    wide = (D_MODEL, D_INNER)
    return pl.pallas_call(
        _ssd_prompt_kernel,
        grid=(nb, seq // t),
        in_specs=[
            pl.BlockSpec((1, t, D_MODEL), lambda b, i: (b, i, 0)),
            _const_spec((1, D_MODEL)),
            _const_spec(wide, (0, 2)), _const_spec(wide, (0, 3)), _const_spec(wide, (0, 4)),
            _const_spec((D_MODEL, DT_PAD)), _const_spec((1, DT_PAD)),
            _const_spec((D_MODEL, D_MODEL), (0, 3)),
            _const_spec((D_INNER, D_MODEL)),
            _const_spec((SSM_CONV, CONV_DIM)), _const_spec((1, CONV_DIM)), _const_spec((1, DT_PAD)),
            _const_spec((1, D_INNER)), _const_spec((1, D_INNER)),
            _const_spec((3 * DT_PAD, D_INNER)),
        ],
        out_specs=[
            pl.BlockSpec((1, t, D_MODEL), lambda b, i: (b, i, 0)),
            pl.BlockSpec((1, SSM_HALO, CONV_DIM), lambda b, i: (b, 0, 0)),
            pl.BlockSpec((1, N_HEADS, HEADDIM, D_STATE), lambda b, i: (b, 0, 0, 0)),
        ],
        out_shape=[
            jax.ShapeDtypeStruct((nb, seq, D_MODEL), F32),
            jax.ShapeDtypeStruct((nb, SSM_HALO, CONV_DIM), F32),
            jax.ShapeDtypeStruct((nb, N_HEADS, HEADDIM, D_STATE), F32),
        ],
        scratch_shapes=[
            pltpu.VMEM((SSM_SLABS, t + SSD_PAD, LANES), F32),
            pltpu.VMEM((t, D_MODEL), BF16),
            pltpu.VMEM((t, D_INNER), F32),
            pltpu.VMEM((t, D_INNER), BF16),
            pltpu.VMEM((t, D_INNER), F32),
            pltpu.VMEM((t, D_INNER), BF16),
            pltpu.VMEM((DT_PAD, CHUNK), F32),
        ],
        compiler_params=_params(2),
        name="ssd_prompt",
    )(x3, g_mix, w_main, w_main, w_main, w_dt, b_dt, w_main, w_ssm, conv_w, conv_b, a_log, d_exp, norm_g,
      ex)


def _tail_prompt_kernel(x_ref, ga_ref, gb_ref, wout_ref, gffn_ref, wup_ref, wdown_ref, gfin_ref, y_ref):
    merged = (ga_ref[...] + gb_ref[...]).astype(BF16)
    h = x_ref[...] + _dot(merged, wout_ref[...])
    y_ref[...] = _mlp_final(h, gffn_ref, wup_ref, wdown_ref, gfin_ref)


def _tail_prompt(x2d, ga, gb, w_out, g_ffn, w_up, w_down, g_final, tm=512):
    m = x2d.shape[0]
    tile = pl.BlockSpec((tm, D_MODEL), lambda i: (i, 0))
    vec = _const_spec((1, D_MODEL))
    return pl.pallas_call(
        _tail_prompt_kernel,
        grid=(m // tm,),
        in_specs=[tile, tile, tile, _const_spec((D_MODEL, D_MODEL)), vec,
                  _const_spec((D_MODEL, D_FF)), _const_spec((D_FF, D_MODEL)), vec],
        out_specs=tile,
        out_shape=jax.ShapeDtypeStruct((m, D_MODEL), F32),
        compiler_params=_params(1),
        name="tail_prompt",
    )(x2d, ga, gb, w_out, g_ffn, w_up, w_down, g_final)


def _inproj_kernel(x_ref, g_ref, w_ref, wdt_ref, bdt_ref, proj_ref, dt_ref, u_scr):
    @pl.when(pl.program_id(0) == 0)
    def _():
        ub = _rmsnorm(x_ref[...], g_ref[...]).astype(BF16)
        u_scr[...] = ub
        dt_ref[...] = _softplus(_dot(ub, wdt_ref[...]) + bdt_ref[...])

    proj_ref[...] = _dot(u_scr[...], w_ref[...])


def _inproj(x2d, g_mix, w_main, w_dt, b_dt, tn=1024):
    m = x2d.shape[0]
    return pl.pallas_call(
        _inproj_kernel,
        grid=(MAIN_COLS // tn,),
        in_specs=[
            pl.BlockSpec((m, D_MODEL), lambda j: (0, 0)),
            pl.BlockSpec((1, D_MODEL), lambda j: (0, 0)),
            pl.BlockSpec((D_MODEL, tn), lambda j: (0, j)),
            pl.BlockSpec((D_MODEL, DT_PAD), lambda j: (0, 0)),
            pl.BlockSpec((1, DT_PAD), lambda j: (0, 0)),
        ],
        out_specs=[
            pl.BlockSpec((m, tn), lambda j: (0, j)),
            pl.BlockSpec((m, DT_PAD), lambda j: (0, 0)),
        ],
        out_shape=[
            jax.ShapeDtypeStruct((m, MAIN_COLS), F32),
            jax.ShapeDtypeStruct((m, DT_PAD), F32),
        ],
        scratch_shapes=[pltpu.VMEM((m, D_MODEL), BF16)],
        compiler_params=_params(1),
        name="inproj_sample",
    )(x2d, g_mix, w_main, w_dt, b_dt)


CONVA_SR = 32
CONVA_SROWS = 40


def _conva_sample_kernel(a_ref, b_ref, st_ref, w_ref, bias_ref, lng_ref, lnb_ref,
                         ca_ref, nst_ref, xe, conv):
    nreq, steps = a_ref.shape[0], a_ref.shape[1]
    xe[:, 32:CONVA_SROWS, :] = jnp.zeros((CONF_SLABS, CONVA_SROWS - 32, LANES), F32)
    per = 8 // steps

    def body(p, carry):
        rows = []
        for q in range(per):
            r = p * per + q
            st = st_ref[r]
            glu = a_ref[r] * _sigmoid(b_ref[r])
            for j in range(CONF_SLABS):
                xe[j, 0:CONF_HALO, :] = st[:, _lanes(j)]
                xe[j, CONF_HALO:CONF_HALO + steps, :] = glu[:, _lanes(j)]
                nst_ref[r, :, _lanes(j)] = xe[j, steps:steps + CONF_HALO, :]
            for t in range(steps):
                rows.append(jnp.concatenate(
                    [jnp.sum(w_ref[:, _lanes(j)] * xe[j, t:t + 32, :], axis=0, keepdims=True)
                     for j in range(CONF_SLABS)], axis=1))
        conv[pl.ds(pl.multiple_of(p * 8, 8), 8), :] = jnp.concatenate(rows, axis=0) + bias_ref[...]
        return carry

    lax.fori_loop(0, nreq // per, body, 0)
    ca_ref[...] = _ln_swish(conv[...], lng_ref[...], lnb_ref[...]).astype(BF16)


def _conva_sample(proj3, state, dw_w32, dw_b, ln_g, ln_b):
    nreq, steps, _ = proj3.shape
    r = CONVA_SR
    vec = pl.BlockSpec((1, C_CONV), lambda i: (0, 0))
    return pl.pallas_call(
        _conva_sample_kernel,
        grid=(nreq // r,),
        in_specs=[
            pl.BlockSpec((r, steps, C_CONV), lambda i: (i, 0, 0)),
            pl.BlockSpec((r, steps, C_CONV), lambda i: (i, 0, 1)),
            pl.BlockSpec((r, CONF_HALO, C_CONV), lambda i: (i, 0, 0)),
            pl.BlockSpec((32, C_CONV), lambda i: (0, 0)),
            vec, vec, vec,
        ],
        out_specs=[
            pl.BlockSpec((r * steps, C_CONV), lambda i: (i, 0)),
            pl.BlockSpec((r, CONF_HALO, C_CONV), lambda i: (i, 0, 0)),
        ],
        out_shape=[
            jax.ShapeDtypeStruct((nreq * steps, C_CONV), BF16),
            jax.ShapeDtypeStruct((nreq, CONF_HALO, C_CONV), F32),
        ],
        scratch_shapes=[pltpu.VMEM((CONF_SLABS, CONVA_SROWS, LANES), F32),
                        pltpu.VMEM((r * steps, C_CONV), F32)],
        compiler_params=_params(1),
        name="conva_sample",
    )(proj3, proj3, state, dw_w32, dw_b, ln_g, ln_b)


SSD_SR = 4
SSD_ST = 8
SSD_SROWS = 16


def _ssd_sample_kernel(z_ref, xr_ref, bcr_ref, dt_ref, cst_ref, h0_ref, cw_ref, cb_ref, alog_ref,
                       dexp_ref, ng_ref, yn_ref, ncv_ref, h_ref, xe, dt8, z8):
    nreq, steps = z_ref.shape[0], z_ref.shape[1]
    t = SSD_ST
    xe[...] = jnp.zeros(xe.shape, F32)
    dt8[...] = jnp.zeros(dt8.shape, F32)
    z8[...] = jnp.zeros(z8.shape, F32)
    rows = lax.broadcasted_iota(jnp.int32, (t, 1), 0)
    live = rows < steps
    lo_half = lax.broadcasted_iota(jnp.int32, (t, LANES), 1) < HEADDIM
    a_neg = -jnp.exp(alog_ref[...])

    def body(r, carry):
        cst = cst_ref[r]
        xr = xr_ref[r]
        bcr = bcr_ref[r]
        for j in range(SSM_SLABS):
            xe[j, 0:SSM_HALO, :] = cst[:, _lanes(j)]
            new = xr[:, _lanes(j)] if j < X_SLABS else bcr[:, _lanes(j - X_SLABS)]
            xe[j, SSM_HALO:SSM_HALO + steps, :] = new
            ncv_ref[r, :, _lanes(j)] = xe[j, steps:steps + SSM_HALO, :]
        dt8[0:steps, :] = dt_ref[r]
        z8[0:steps, :] = z_ref[r]

        def conv(first, count):
            tiles = []
            for j in range(first, first + count):
                acc = jnp.broadcast_to(cb_ref[:, _lanes(j)], (t, LANES))
                for k in range(SSM_CONV):
                    acc = acc + cw_ref[k:k + 1, _lanes(j)] * xe[j, k:k + t, :]
                tiles.append(jnp.where(live, _silu(acc), 0.0))
            return tiles[0] if count == 1 else jnp.concatenate(tiles, axis=1)

        dt = dt8[...]
        a = dt * a_neg
        acum = jnp.zeros((t, DT_PAD), F32)
        for s in range(steps):
            acum = acum + jnp.where(rows >= s, a[s:s + 1, :], 0.0)
        a_last = acum[t - 1:t, :]
        eac = jnp.exp(acum)
        wsd = jnp.exp(a_last - acum)
        ealast = jnp.exp(a_last)

        for g in range(N_GROUPS):
            gs = slice(g * GROUP_W, (g + 1) * GROUP_W)
            xs = conv(2 * g, 2)
            bm = conv(X_SLABS + g, 1)
            cm = conv(X_SLABS + N_GROUPS + g, 1)
            scores = _dot_nt(cm, bm)
            h_g = h0_ref[r, 4 * g:4 * g + 4].reshape(GROUP_W, D_STATE)
            xdt = xs * _expand_heads(dt, g, lo_half)
            acx = _expand_heads(acum, g, lo_half)
            y = _dot_nt(cm, h_g) * _expand_heads(eac, g, lo_half) + xs * dexp_ref[:, gs]
            for s in range(steps):
                decay = jnp.exp(jnp.where(rows >= s, acx - acx[s:s + 1, :], -jnp.inf))
                y = y + (scores[:, s:s + 1] * decay) * xdt[s:s + 1, :]
            upd = _dot_tn(xdt * _expand_heads(wsd, g, lo_half), bm)
            for e in range(HEADS_PER_GROUP):
                hd = 4 * g + e
                h_ref[r, hd] = h0_ref[r, hd] * ealast[:, hd:hd + 1] + upd[e * HEADDIM:(e + 1) * HEADDIM, :]
            yn = _gated_group_norm(y, z8[:, gs], ng_ref[:, gs])
            yn_ref[r, :, gs] = yn[0:steps, :]
        return carry

    lax.fori_loop(0, nreq, body, 0)


def _ssd_sample(proj3, dt3, conv_state, h0, conv_w, conv_b, a_log, d_exp, norm_g):
    nreq, steps, _ = proj3.shape
    r = SSD_SR
    return pl.pallas_call(
        _ssd_sample_kernel,
        grid=(nreq // r,),
        in_specs=[
            pl.BlockSpec((r, steps, D_INNER), lambda i: (i, 0, 2)),
            pl.BlockSpec((r, steps, D_INNER), lambda i: (i, 0, 3)),
            pl.BlockSpec((r, steps, D_INNER), lambda i: (i, 0, 4)),
            pl.BlockSpec((r, steps, DT_PAD), lambda i: (i, 0, 0)),
            pl.BlockSpec((r, SSM_HALO, CONV_DIM), lambda i: (i, 0, 0)),
            pl.BlockSpec((r, N_HEADS, HEADDIM, D_STATE), lambda i: (i, 0, 0, 0)),
            pl.BlockSpec((SSM_CONV, CONV_DIM), lambda i: (0, 0)),
            pl.BlockSpec((1, CONV_DIM), lambda i: (0, 0)),
            pl.BlockSpec((1, DT_PAD), lambda i: (0, 0)),
            pl.BlockSpec((1, D_INNER), lambda i: (0, 0)),
            pl.BlockSpec((1, D_INNER), lambda i: (0, 0)),
        ],
        out_specs=[
            pl.BlockSpec((r, steps, D_INNER), lambda i: (i, 0, 0)),
            pl.BlockSpec((r, SSM_HALO, CONV_DIM), lambda i: (i, 0, 0)),
            pl.BlockSpec((r, N_HEADS, HEADDIM, D_STATE), lambda i: (i, 0, 0, 0)),
        ],
        out_shape=[
            jax.ShapeDtypeStruct((nreq, steps, D_INNER), F32),
            jax.ShapeDtypeStruct((nreq, SSM_HALO, CONV_DIM), F32),
            jax.ShapeDtypeStruct((nreq, N_HEADS, HEADDIM, D_STATE), F32),
        ],
        scratch_shapes=[
            pltpu.VMEM((SSM_SLABS, SSD_SROWS, LANES), F32),
            pltpu.VMEM((SSD_ST, DT_PAD), F32),
            pltpu.VMEM((SSD_ST, D_INNER), F32),
        ],
        compiler_params=_params(1),
        name="ssd_sample",
    )(proj3, proj3, proj3, dt3, conv_state, h0, conv_w, conv_b, a_log, d_exp, norm_g)


def _tail_sample_kernel(x_ref, ca_ref, yn_ref, gate_ref, wpw_ref, bpw_ref, wssm_ref, wout_ref, gffn_ref,
                        wup_ref, wdown_ref, gfin_ref, y_ref):
    branch_a = _dot(ca_ref[...], wpw_ref[...]) + bpw_ref[...]
    branch_b = _dot(yn_ref[...].astype(BF16), wssm_ref[...])
    merged = (_sigmoid(gate_ref[:, 0:D_MODEL]) * branch_a
              + _sigmoid(gate_ref[:, D_MODEL:2 * D_MODEL]) * branch_b)
    h = x_ref[...] + _dot(merged.astype(BF16), wout_ref[...])
    y_ref[...] = _mlp_final(h, gffn_ref, wup_ref, wdown_ref, gfin_ref)


def _tail_sample(x2d, ca, yn, proj2d, w_pw, b_pw, w_ssm, w_out, g_ffn, w_up, w_down, g_final):
    m = x2d.shape[0]
    vec = _const_spec((1, D_MODEL))
    return pl.pallas_call(
        _tail_sample_kernel,
        grid=(1,),
        in_specs=[
            pl.BlockSpec((m, D_MODEL), lambda i: (0, 0)),
            pl.BlockSpec((m, C_CONV), lambda i: (0, 0)),
            pl.BlockSpec((m, D_INNER), lambda i: (0, 0)),
            pl.BlockSpec((m, 2 * D_MODEL), lambda i: (0, 1)),
            _const_spec((C_CONV, D_MODEL)), vec, _const_spec((D_INNER, D_MODEL)),
            _const_spec((D_MODEL, D_MODEL)), vec, _const_spec((D_MODEL, D_FF)),
            _const_spec((D_FF, D_MODEL)), vec,
        ],
        out_specs=pl.BlockSpec((m, D_MODEL), lambda i: (0, 0)),
        out_shape=jax.ShapeDtypeStruct((m, D_MODEL), F32),
        compiler_params=_params(1),
        name="tail_sample",
    )(x2d, ca, yn, proj2d, w_pw, b_pw, w_ssm, w_out, g_ffn, w_up, w_down, g_final)


def kernel(x_prompt, x_sample, state_conf_conv, state_ssm_conv, state_ssm, g_mix, w_in, conf_dw_w,
           conf_dw_b, conf_ln_g, conf_ln_b, conf_w_pw, conf_b_pw, ssm_conv_w, ssm_conv_b, ssm_dt_bias,
           ssm_a_log, ssm_d, ssm_norm_g, ssm_w_out, w_out, g_ffn, w_up, w_down, g_final):
    depth = w_in.shape[0]
    assert depth == 1
    nb, seq, _ = x_prompt.shape
    nreq, steps, _ = x_sample.shape
    i = 0

    def row(v):
        return v.reshape(1, -1)

    def pad_lanes(v, width):
        return jnp.pad(v, ((0, 0), (0, width - v.shape[1])))

    gm = row(g_mix[i])
    w_main = w_in[i][:, :MAIN_COLS].astype(BF16)
    w_dt = pad_lanes(w_in[i][:, MAIN_COLS:], DT_PAD).astype(BF16)
    b_dt = pad_lanes(row(ssm_dt_bias[i]), DT_PAD)
    a_log = pad_lanes(row(ssm_a_log[i]), DT_PAD)
    d_exp = row(jnp.repeat(ssm_d[i], HEADDIM))
    norm_g = row(ssm_norm_g[i])
    dw_w = conf_dw_w[i]
    dw_w32 = jnp.pad(dw_w, ((0, 32 - CONF_KERNEL), (0, 0)))
    dw_b, ln_g, ln_b = row(conf_dw_b[i]), row(conf_ln_g[i]), row(conf_ln_b[i])
    cw, cb = ssm_conv_w[i], row(ssm_conv_b[i])
    w_pw, b_pw = conf_w_pw[i].astype(BF16), row(conf_b_pw[i])
    w_ssm = ssm_w_out[i].astype(BF16)
    mlp_w = (w_out[i].astype(BF16), row(g_ffn[i]), w_up[i].astype(BF16), w_down[i].astype(BF16),
             row(g_final))

    ga, conf_p = _conva_prompt(x_prompt, gm, w_main, dw_w, dw_b, ln_g, ln_b, w_pw, b_pw)
    head_row = jnp.arange(3 * DT_PAD, dtype=jnp.int32)[:, None] % DT_PAD
    ex = (head_row == jnp.arange(D_INNER, dtype=jnp.int32)[None, :] // HEADDIM).astype(BF16)
    gb, scv_p, h_p = _ssd_prompt(x_prompt, gm, w_main, w_dt, b_dt, w_ssm, cw, cb, a_log, d_exp, norm_g, ex)
    m_p = nb * seq
    y_p = _tail_prompt(x_prompt.reshape(m_p, D_MODEL), ga.reshape(m_p, D_MODEL), gb.reshape(m_p, D_MODEL),
                       *mlp_w)

    xs = x_sample.reshape(nreq * steps, D_MODEL)
    proj_s, dt_s = _inproj(xs, gm, w_main, w_dt, b_dt)
    proj_s3 = proj_s.reshape(nreq, steps, MAIN_COLS)
    ca_s, conf_s = _conva_sample(proj_s3, state_conf_conv[i], dw_w32, dw_b, ln_g, ln_b)
    yn_s, scv_s, h_s = _ssd_sample(proj_s3, dt_s.reshape(nreq, steps, DT_PAD), state_ssm_conv[i],
                                   state_ssm[i], cw, cb, a_log, d_exp, norm_g)
    y_s = _tail_sample(xs, ca_s, yn_s.reshape(nreq * steps, D_INNER), proj_s, w_pw, b_pw, w_ssm, *mlp_w)

    return (y_p.reshape(nb, seq, D_MODEL), y_s.reshape(nreq, steps, D_MODEL),
            conf_p[None], scv_p[None], h_p[None], conf_s[None], scv_s[None], h_s[None])
```

```python
import jax
import jax.numpy as jnp
from jax import lax
from jax.experimental import pallas as pl
from jax.experimental.pallas import tpu as pltpu

F32 = jnp.float32
BF16 = jnp.bfloat16

D_MODEL = 1024
C_CONV = 1024
CONF_KERNEL = 31
CONF_HALO = CONF_KERNEL - 1
D_INNER = 2048
HEADDIM = 64
N_HEADS = 32
N_GROUPS = 8
HEADS_PER_GROUP = 4
GROUP_W = HEADS_PER_GROUP * HEADDIM
D_STATE = 128
SSM_CONV = 4
SSM_HALO = SSM_CONV - 1
CONV_DIM = 4096
CHUNK = 128
D_FF = 4096
EPS = 1e-6
MAIN_COLS = 10240
DT_PAD = 128
LANES = 128
CONF_SLABS = C_CONV // LANES
SSM_SLABS = CONV_DIM // LANES
X_SLABS = D_INNER // LANES
VMEM_LIMIT = 56 * 1024 * 1024


def _sigmoid(x):
    return jax.nn.sigmoid(x)


def _silu(x):
    return x * jax.nn.sigmoid(x)


def _softplus(x):
    return jnp.maximum(x, 0.0) + jnp.log1p(jnp.exp(-jnp.abs(x)))


def _rmsnorm(x, g):
    return x * lax.rsqrt(jnp.mean(x * x, axis=-1, keepdims=True) + EPS) * g


def _dot(a, b):
    return jnp.dot(a, b, preferred_element_type=F32)


def _dot_nt(a, b):
    return lax.dot_general(a, b, (((1,), (1,)), ((), ())), preferred_element_type=F32)


def _dot_tn(a, b):
    return lax.dot_general(a, b, (((0,), (0,)), ((), ())), preferred_element_type=F32)


def _split3(x):
    hi = x.astype(BF16)
    r1 = x - hi.astype(F32)
    mid = r1.astype(BF16)
    lo = (r1 - mid.astype(F32)).astype(BF16)
    return hi, mid, lo


def _cumsum_rows(a, tri):
    hi, mid, lo = _split3(a)
    return _dot(tri, hi) + _dot(tri, mid) + _dot(tri, lo)


def _lanes(j):
    return slice(j * LANES, (j + 1) * LANES)


def _const_spec(shape, index=None):
    nd = len(shape)
    idx = index if index is not None else (0,) * nd
    return pl.BlockSpec(shape, lambda *_: idx, pipeline_mode=pl.Buffered(1))


def _params(ngrid):
    return pltpu.CompilerParams(dimension_semantics=("arbitrary",) * ngrid, vmem_limit_bytes=VMEM_LIMIT)


def _ln_swish(y, g, b):
    mu = jnp.mean(y, axis=-1, keepdims=True)
    yc = y - mu
    yn = yc * lax.rsqrt(jnp.mean(yc * yc, axis=-1, keepdims=True) + EPS) * g + b
    return _silu(yn)


def _mlp_final(h, gffn_ref, wup_ref, wdown_ref, gfin_ref):
    hidden = jnp.square(jnp.maximum(_dot(_rmsnorm(h, gffn_ref[...]).astype(BF16), wup_ref[...]), 0.0))
    h = h + _dot(hidden.astype(BF16), wdown_ref[...])
    return _rmsnorm(h, gfin_ref[...])


CONVA_T = 512
CONVA_RB = 64
CONVA_PAD = 32


def _conva_prompt_kernel(x_ref, gmix_ref, wa_ref, wb_ref, wga_ref, dw_ref, dwb_ref, lng_ref, lnb_ref,
                         wpw_ref, bpw_ref, ga_ref, st_ref, xe, conv, u_scr):
    t = CONVA_T

    @pl.when(pl.program_id(1) == 0)
    def _():
        xe[:, 0:CONVA_PAD, :] = jnp.zeros((CONF_SLABS, CONVA_PAD, LANES), F32)

    u_scr[...] = _rmsnorm(x_ref[0], gmix_ref[...]).astype(BF16)
    off = CONVA_PAD - CONF_HALO
    glu = _dot(u_scr[...], wa_ref[...]) * _sigmoid(_dot(u_scr[...], wb_ref[...]))
    for j in range(CONF_SLABS):
        xe[j, CONVA_PAD:CONVA_PAD + t, :] = glu[:, _lanes(j)]
    for j in range(CONF_SLABS):
        for rb in range(t // CONVA_RB):
            r0 = rb * CONVA_RB
            acc = jnp.broadcast_to(dwb_ref[:, _lanes(j)], (CONVA_RB, LANES))
            for k in range(CONF_KERNEL):
                acc = acc + dw_ref[k:k + 1, _lanes(j)] * xe[j, r0 + off + k:r0 + off + k + CONVA_RB, :]
            conv[r0:r0 + CONVA_RB, _lanes(j)] = acc
    ca = _ln_swish(conv[...], lng_ref[...], lnb_ref[...]).astype(BF16)
    ga_ref[0] = _sigmoid(_dot(u_scr[...], wga_ref[...])) * (_dot(ca, wpw_ref[...]) + bpw_ref[...])
    for j in range(CONF_SLABS):
        st_ref[0, :, _lanes(j)] = xe[j, t + off:t + CONVA_PAD, :]
        xe[j, 0:CONVA_PAD, :] = xe[j, t:t + CONVA_PAD, :]


def _conva_prompt(x3, g_mix, w_main, dw_w, dw_b, ln_g, ln_b, w_pw, b_pw):
    nb, seq, _ = x3.shape
    t = CONVA_T
    vec = _const_spec((1, C_CONV))
    sq = (D_MODEL, D_MODEL)
    return pl.pallas_call(
        _conva_prompt_kernel,
        grid=(nb, seq // t),
        in_specs=[
            pl.BlockSpec((1, t, D_MODEL), lambda b, i: (b, i, 0)),
            vec,
            _const_spec(sq, (0, 0)), _const_spec(sq, (0, 1)), _const_spec(sq, (0, 2)),
            _const_spec((CONF_KERNEL, C_CONV)), vec, vec, vec,
            _const_spec(sq), vec,
        ],
        out_specs=[
            pl.BlockSpec((1, t, D_MODEL), lambda b, i: (b, i, 0)),
            pl.BlockSpec((1, CONF_HALO, C_CONV), lambda b, i: (b, 0, 0)),
        ],
        out_shape=[
            jax.ShapeDtypeStruct((nb, seq, D_MODEL), F32),
            jax.ShapeDtypeStruct((nb, CONF_HALO, C_CONV), F32),
        ],
        scratch_shapes=[
            pltpu.VMEM((CONF_SLABS, t + CONVA_PAD, LANES), F32),
            pltpu.VMEM((t, C_CONV), F32),
            pltpu.VMEM((t, D_MODEL), BF16),
        ],
        compiler_params=_params(2),
        name="conva_prompt",
    )(x3, g_mix, w_main, w_main, w_main, dw_w, dw_b, ln_g, ln_b, w_pw, b_pw)


def _expand_heads(v, g, lo_half):
    rows = v.shape[0]

    def col(e):
        return jnp.broadcast_to(v[:, 4 * g + e:4 * g + e + 1], (rows, LANES))

    return jnp.concatenate([jnp.where(lo_half, col(0), col(1)), jnp.where(lo_half, col(2), col(3))], axis=1)


def _gated_group_norm(y, z, g):
    y = y * _silu(z)
    return y * lax.rsqrt(jnp.mean(y * y, axis=-1, keepdims=True) + EPS) * g


LOG2E = 1.4426950408889634


def _cat3(v):
    return jnp.concatenate(_split3(v), axis=1)


def _ssd_chunk(rows, dt, xs_scr, bc_scr, z_scr, yn_scr, h_ref, a_neg, dexp_ref, ng_ref, ex_ref, acumt_scr):
    t = CHUNK
    r_i = lax.broadcasted_iota(jnp.int32, (t, t), 0)
    c_i = lax.broadcasted_iota(jnp.int32, (t, t), 1)
    causal = r_i >= c_i
    tri = causal.astype(BF16)
    lo_half = lax.broadcasted_iota(jnp.int32, (t, LANES), 1) < HEADDIM
    zeros = jnp.zeros((t, LANES), F32)
    zeros_b = jnp.zeros((t, t), BF16)

    acum = _cumsum_rows(dt * a_neg, tri)
    a_last = acum[t - 1:t, :]
    ealast = jnp.exp(a_last)
    acum2 = acum * LOG2E
    acumt_scr[...] = acum2.T
    dt3 = _cat3(dt)

    def stage_a(g):
        gs = slice(g * GROUP_W, (g + 1) * GROUP_W)
        xs = xs_scr[rows, gs]
        bm = bc_scr[rows, _lanes(g)]
        cm = bc_scr[rows, _lanes(N_GROUPS + g)]
        scores = _dot_nt(cm, bm)
        h_g = h_ref[0, 4 * g:4 * g + 4].reshape(GROUP_W, D_STATE)
        colb = [jnp.broadcast_to(acum2[:, 4 * g + e:4 * g + e + 1], (t, LANES))
                for e in range(HEADS_PER_GROUP)]
        acx = jnp.concatenate([jnp.where(lo_half, colb[0], colb[1]),
                               jnp.where(lo_half, colb[2], colb[3])], axis=1)
        xdt = xs * _dot(dt3, ex_ref[:, gs])
        y = _dot_nt(cm, h_g.astype(BF16)) * jnp.exp2(acx) + xs * dexp_ref[:, gs]
        ms, xblk = [], []
        for e in range(HEADS_PER_GROUP):
            hd = 4 * g + e
            seg = colb[e] - acumt_scr[hd:hd + 1, :]
            m = (scores * jnp.exp2(seg)).astype(BF16)
            ms.append(jnp.where(causal, m, zeros_b))
            half = xdt[:, _lanes(e // 2)]
            keep = jnp.where(lo_half, half, 0.0) if e % 2 == 0 else jnp.where(lo_half, 0.0, half)
            blk = jnp.concatenate([keep, zeros] if e < 2 else [zeros, keep], axis=1)
            xblk.append(blk.astype(BF16))
        xw = (xdt * jnp.exp2(acx[t - 1:t, :] - acx)).astype(BF16)
        return y, jnp.concatenate(ms, axis=1), jnp.concatenate(xblk, axis=0), xw, bm

    def stage_b(g, staged):
        y, mcat, xcat, xw, bm = staged
        gs = slice(g * GROUP_W, (g + 1) * GROUP_W)
        y = y + _dot(mcat, xcat)
        upd = _dot_tn(xw, bm)
        for e in range(HEADS_PER_GROUP):
            hd = 4 * g + e
            h_ref[0, hd] = h_ref[0, hd] * ealast[:, hd:hd + 1] + upd[e * HEADDIM:(e + 1) * HEADDIM, :]
        yn_scr[rows, gs] = _gated_group_norm(y, z_scr[rows, gs], ng_ref[:, gs]).astype(BF16)

    staged = stage_a(0)
    for g in range(N_GROUPS):
        following = stage_a(g + 1) if g + 1 < N_GROUPS else None
        stage_b(g, staged)
        staged = following


SSDP_T = 256
SSDP_RB = 64
SSD_PAD = 8


def _ssd_prompt_kernel(x_ref, gmix_ref, wz_ref, wx_ref, wbc_ref, wdt_ref, bdt_ref, wgb_ref, wssm_ref,
                       cw_ref, cb_ref, alog_ref, dexp_ref, ng_ref, ex_ref,
                       gb_ref, ncv_ref, h_ref,
                       xe, u_scr, xs_scr, bc_scr, z_scr, yn_scr, acumt_scr):
    t = SSDP_T

    @pl.when(pl.program_id(1) == 0)
    def _():
        xe[:, 0:SSD_PAD, :] = jnp.zeros((SSM_SLABS, SSD_PAD, LANES), F32)
        h_ref[...] = jnp.zeros(h_ref.shape, F32)

    u_scr[...] = _rmsnorm(x_ref[0], gmix_ref[...]).astype(BF16)
    a_neg = -jnp.exp(alog_ref[...])
    off = SSD_PAD - SSM_HALO
    for c in range(t // CHUNK):
        c0 = c * CHUNK
        rows = slice(c0, c0 + CHUNK)
        u_c = u_scr[rows, :]
        z_scr[rows, :] = _dot(u_c, wz_ref[...])
        dt = _softplus(_dot(u_c, wdt_ref[...]) + bdt_ref[...])
        xr = _dot(u_c, wx_ref[...])
        for j in range(X_SLABS):
            xe[j, SSD_PAD + c0:SSD_PAD + c0 + CHUNK, :] = xr[:, _lanes(j)]
        bcr = _dot(u_c, wbc_ref[...])
        for j in range(X_SLABS):
            xe[X_SLABS + j, SSD_PAD + c0:SSD_PAD + c0 + CHUNK, :] = bcr[:, _lanes(j)]
        for j in range(SSM_SLABS):
            for rb in range(CHUNK // SSDP_RB):
                r0 = c0 + rb * SSDP_RB
                acc = jnp.broadcast_to(cb_ref[:, _lanes(j)], (SSDP_RB, LANES))
                for k in range(SSM_CONV):
                    acc = acc + cw_ref[k:k + 1, _lanes(j)] * xe[j, r0 + off + k:r0 + off + k + SSDP_RB, :]
                if j < X_SLABS:
                    xs_scr[r0:r0 + SSDP_RB, _lanes(j)] = _silu(acc)
                else:
                    bc_scr[r0:r0 + SSDP_RB, _lanes(j - X_SLABS)] = _silu(acc).astype(BF16)
        _ssd_chunk(rows, dt, xs_scr, bc_scr, z_scr, yn_scr, h_ref, a_neg, dexp_ref, ng_ref, ex_ref, acumt_scr)
        gb_ref[0, rows, :] = _sigmoid(_dot(u_c, wgb_ref[...])) * _dot(yn_scr[rows, :], wssm_ref[...])
    for j in range(SSM_SLABS):
        ncv_ref[0, :, _lanes(j)] = xe[j, t + off:t + SSD_PAD, :]
        xe[j, 0:SSD_PAD, :] = xe[j, t:t + SSD_PAD, :]


def _ssd_prompt(x3, g_mix, w_main, w_dt, b_dt, w_ssm, conv_w, conv_b, a_log, d_exp, norm_g, ex):
    nb, seq, _ = x3.shape
    t = SSDP_T
    wide = (D_MODEL, D_INNER)
    return pl.pallas_call(
        _ssd_prompt_kernel,
        grid=(nb, seq // t),
        in_specs=[
            pl.BlockSpec((1, t, D_MODEL), lambda b, i: (b, i, 0)),
            _const_spec((1, D_MODEL)),
            _const_spec(wide, (0, 2)), _const_spec(wide, (0, 3)), _const_spec(wide, (0, 4)),
            _const_spec((D_MODEL, DT_PAD)), _const_spec((1, DT_PAD)),
            _const_spec((D_MODEL, D_MODEL), (0, 3)),
            _const_spec((D_INNER, D_MODEL)),
            _const_spec((SSM_CONV, CONV_DIM)), _const_spec((1, CONV_DIM)), _const_spec((1, DT_PAD)),
            _const_spec((1, D_INNER)), _const_spec((1, D_INNER)),
            _const_spec((3 * DT_PAD, D_INNER)),
        ],
        out_specs=[
            pl.BlockSpec((1, t, D_MODEL), lambda b, i: (b, i, 0)),
            pl.BlockSpec((1, SSM_HALO, CONV_DIM), lambda b, i: (b, 0, 0)),
            pl.BlockSpec((1, N_HEADS, HEADDIM, D_STATE), lambda b, i: (b, 0, 0, 0)),
        ],
        out_shape=[
            jax.ShapeDtypeStruct((nb, seq, D_MODEL), F32),
            jax.ShapeDtypeStruct((nb, SSM_HALO, CONV_DIM), F32),
            jax.ShapeDtypeStruct((nb, N_HEADS, HEADDIM, D_STATE), F32),
        ],
        scratch_shapes=[
            pltpu.VMEM((SSM_SLABS, t + SSD_PAD, LANES), F32),
            pltpu.VMEM((t, D_MODEL), BF16),
            pltpu.VMEM((t, D_INNER), F32),
            pltpu.VMEM((t, D_INNER), BF16),
            pltpu.VMEM((t, D_INNER), F32),
            pltpu.VMEM((t, D_INNER), BF16),
            pltpu.VMEM((DT_PAD, CHUNK), F32),
        ],
        compiler_params=_params(2),
        name="ssd_prompt",
    )(x3, g_mix, w_main, w_main, w_main, w_dt, b_dt, w_main, w_ssm, conv_w, conv_b, a_log, d_exp, norm_g,
      ex)


def _tail_prompt_kernel(x_ref, ga_ref, gb_ref, wout_ref, gffn_ref, wup_ref, wdown_ref, gfin_ref, y_ref):
    merged = (ga_ref[...] + gb_ref[...]).astype(BF16)
    h = x_ref[...] + _dot(merged, wout_ref[...])
    y_ref[...] = _mlp_final(h, gffn_ref, wup_ref, wdown_ref, gfin_ref)


def _tail_prompt(x2d, ga, gb, w_out, g_ffn, w_up, w_down, g_final, tm=512):
    m = x2d.shape[0]
    tile = pl.BlockSpec((tm, D_MODEL), lambda i: (i, 0))
    vec = _const_spec((1, D_MODEL))
    return pl.pallas_call(
        _tail_prompt_kernel,
        grid=(m // tm,),
        in_specs=[tile, tile, tile, _const_spec((D_MODEL, D_MODEL)), vec,
                  _const_spec((D_MODEL, D_FF)), _const_spec((D_FF, D_MODEL)), vec],
        out_specs=tile,
        out_shape=jax.ShapeDtypeStruct((m, D_MODEL), F32),
        compiler_params=_params(1),
        name="tail_prompt",
    )(x2d, ga, gb, w_out, g_ffn, w_up, w_down, g_final)


def _inproj_kernel(x_ref, g_ref, w_ref, wdt_ref, bdt_ref, proj_ref, dt_ref, u_scr):
    @pl.when(pl.program_id(0) == 0)
    def _():
        ub = _rmsnorm(x_ref[...], g_ref[...]).astype(BF16)
        u_scr[...] = ub
        dt_ref[...] = _softplus(_dot(ub, wdt_ref[...]) + bdt_ref[...])

    proj_ref[...] = _dot(u_scr[...], w_ref[...])


def _inproj(x2d, g_mix, w_main, w_dt, b_dt, tn=1024):
    m = x2d.shape[0]
    return pl.pallas_call(
        _inproj_kernel,
        grid=(MAIN_COLS // tn,),
        in_specs=[
            pl.BlockSpec((m, D_MODEL), lambda j: (0, 0)),
            pl.BlockSpec((1, D_MODEL), lambda j: (0, 0)),
            pl.BlockSpec((D_MODEL, tn), lambda j: (0, j)),
            pl.BlockSpec((D_MODEL, DT_PAD), lambda j: (0, 0)),
            pl.BlockSpec((1, DT_PAD), lambda j: (0, 0)),
        ],
        out_specs=[
            pl.BlockSpec((m, tn), lambda j: (0, j)),
            pl.BlockSpec((m, DT_PAD), lambda j: (0, 0)),
        ],
        out_shape=[
            jax.ShapeDtypeStruct((m, MAIN_COLS), F32),
            jax.ShapeDtypeStruct((m, DT_PAD), F32),
        ],
        scratch_shapes=[pltpu.VMEM((m, D_MODEL), BF16)],
        compiler_params=_params(1),
        name="inproj_sample",
    )(x2d, g_mix, w_main, w_dt, b_dt)


CONVA_SR = 32
CONVA_SROWS = 40


def _conva_sample_kernel(a_ref, b_ref, st_ref, w_ref, bias_ref, lng_ref, lnb_ref,
                         ca_ref, nst_ref, xe, conv):
    nreq, steps = a_ref.shape[0], a_ref.shape[1]
    xe[:, 32:CONVA_SROWS, :] = jnp.zeros((CONF_SLABS, CONVA_SROWS - 32, LANES), F32)
    per = 8 // steps

    def body(p, carry):
        rows = []
        for q in range(per):
            r = p * per + q
            st = st_ref[r]
            glu = a_ref[r] * _sigmoid(b_ref[r])
            for j in range(CONF_SLABS):
                xe[j, 0:CONF_HALO, :] = st[:, _lanes(j)]
                xe[j, CONF_HALO:CONF_HALO + steps, :] = glu[:, _lanes(j)]
                nst_ref[r, :, _lanes(j)] = xe[j, steps:steps + CONF_HALO, :]
            for t in range(steps):
                rows.append(jnp.concatenate(
                    [jnp.sum(w_ref[:, _lanes(j)] * xe[j, t:t + 32, :], axis=0, keepdims=True)
                     for j in range(CONF_SLABS)], axis=1))
        conv[pl.ds(pl.multiple_of(p * 8, 8), 8), :] = jnp.concatenate(rows, axis=0) + bias_ref[...]
        return carry

    lax.fori_loop(0, nreq // per, body, 0)
    ca_ref[...] = _ln_swish(conv[...], lng_ref[...], lnb_ref[...]).astype(BF16)


def _conva_sample(proj3, state, dw_w32, dw_b, ln_g, ln_b):
    nreq, steps, _ = proj3.shape
    r = CONVA_SR
    vec = pl.BlockSpec((1, C_CONV), lambda i: (0, 0))
    return pl.pallas_call(
        _conva_sample_kernel,
        grid=(nreq // r,),
        in_specs=[
            pl.BlockSpec((r, steps, C_CONV), lambda i: (i, 0, 0)),
            pl.BlockSpec((r, steps, C_CONV), lambda i: (i, 0, 1)),
            pl.BlockSpec((r, CONF_HALO, C_CONV), lambda i: (i, 0, 0)),
            pl.BlockSpec((32, C_CONV), lambda i: (0, 0)),
            vec, vec, vec,
        ],
        out_specs=[
            pl.BlockSpec((r * steps, C_CONV), lambda i: (i, 0)),
            pl.BlockSpec((r, CONF_HALO, C_CONV), lambda i: (i, 0, 0)),
        ],
        out_shape=[
            jax.ShapeDtypeStruct((nreq * steps, C_CONV), BF16),
            jax.ShapeDtypeStruct((nreq, CONF_HALO, C_CONV), F32),
        ],
        scratch_shapes=[pltpu.VMEM((CONF_SLABS, CONVA_SROWS, LANES), F32),
                        pltpu.VMEM((r * steps, C_CONV), F32)],
        compiler_params=_params(1),
        name="conva_sample",
    )(proj3, proj3, state, dw_w32, dw_b, ln_g, ln_b)


SSD_SR = 4
SSD_ST = 8
SSD_SROWS = 16


def _ssd_sample_kernel(z_ref, xr_ref, bcr_ref, dt_ref, cst_ref, h0_ref, cw_ref, cb_ref, alog_ref,
                       dexp_ref, ng_ref, yn_ref, ncv_ref, h_ref, xe, dt8, z8):
    nreq, steps = z_ref.shape[0], z_ref.shape[1]
    t = SSD_ST
    xe[...] = jnp.zeros(xe.shape, F32)
    dt8[...] = jnp.zeros(dt8.shape, F32)
    z8[...] = jnp.zeros(z8.shape, F32)
    rows = lax.broadcasted_iota(jnp.int32, (t, 1), 0)
    live = rows < steps
    lo_half = lax.broadcasted_iota(jnp.int32, (t, LANES), 1) < HEADDIM
    a_neg = -jnp.exp(alog_ref[...])

    def body(r, carry):
        cst = cst_ref[r]
        xr = xr_ref[r]
        bcr = bcr_ref[r]
        for j in range(SSM_SLABS):
            xe[j, 0:SSM_HALO, :] = cst[:, _lanes(j)]
            new = xr[:, _lanes(j)] if j < X_SLABS else bcr[:, _lanes(j - X_SLABS)]
            xe[j, SSM_HALO:SSM_HALO + steps, :] = new
            ncv_ref[r, :, _lanes(j)] = xe[j, steps:steps + SSM_HALO, :]
        dt8[0:steps, :] = dt_ref[r]
        z8[0:steps, :] = z_ref[r]

        def conv(first, count):
            tiles = []
            for j in range(first, first + count):
                acc = jnp.broadcast_to(cb_ref[:, _lanes(j)], (t, LANES))
                for k in range(SSM_CONV):
                    acc = acc + cw_ref[k:k + 1, _lanes(j)] * xe[j, k:k + t, :]
                tiles.append(jnp.where(live, _silu(acc), 0.0))
            return tiles[0] if count == 1 else jnp.concatenate(tiles, axis=1)

        dt = dt8[...]
        a = dt * a_neg
        acum = jnp.zeros((t, DT_PAD), F32)
        for s in range(steps):
            acum = acum + jnp.where(rows >= s, a[s:s + 1, :], 0.0)
        a_last = acum[t - 1:t, :]
        eac = jnp.exp(acum)
        wsd = jnp.exp(a_last - acum)
        ealast = jnp.exp(a_last)

        def stage_a(g):
            gs = slice(g * GROUP_W, (g + 1) * GROUP_W)
            xs = conv(2 * g, 2)
            bm = conv(X_SLABS + g, 1)
            cm = conv(X_SLABS + N_GROUPS + g, 1)
            scores = _dot_nt(cm, bm)
            h_g = h0_ref[r, 4 * g:4 * g + 4].reshape(GROUP_W, D_STATE)
            xdt = xs * _expand_heads(dt, g, lo_half)
            acx = _expand_heads(acum, g, lo_half)
            y = _dot_nt(cm, h_g) * _expand_heads(eac, g, lo_half) + xs * dexp_ref[:, gs]
            for s in range(steps):
                decay = jnp.exp(jnp.where(rows >= s, acx - acx[s:s + 1, :], -jnp.inf))
                y = y + (scores[:, s:s + 1] * decay) * xdt[s:s + 1, :]
            return y, xdt * _expand_heads(wsd, g, lo_half), bm

        def stage_b(g, staged):
            y, xw, bm = staged
            gs = slice(g * GROUP_W, (g + 1) * GROUP_W)
            upd = _dot_tn(xw, bm)
            for e in range(HEADS_PER_GROUP):
                hd = 4 * g + e
                h_ref[r, hd] = h0_ref[r, hd] * ealast[:, hd:hd + 1] + upd[e * HEADDIM:(e + 1) * HEADDIM, :]
            yn = _gated_group_norm(y, z8[:, gs], ng_ref[:, gs])
            yn_ref[r, :, gs] = yn[0:steps, :]

        staged = stage_a(0)
        for g in range(N_GROUPS):
            following = stage_a(g + 1) if g + 1 < N_GROUPS else None
            stage_b(g, staged)
            staged = following
        return carry

    lax.fori_loop(0, nreq, body, 0)


def _ssd_sample(proj3, dt3, conv_state, h0, conv_w, conv_b, a_log, d_exp, norm_g):
    nreq, steps, _ = proj3.shape
    r = SSD_SR
    return pl.pallas_call(
        _ssd_sample_kernel,
        grid=(nreq // r,),
        in_specs=[
            pl.BlockSpec((r, steps, D_INNER), lambda i: (i, 0, 2)),
            pl.BlockSpec((r, steps, D_INNER), lambda i: (i, 0, 3)),
            pl.BlockSpec((r, steps, D_INNER), lambda i: (i, 0, 4)),
            pl.BlockSpec((r, steps, DT_PAD), lambda i: (i, 0, 0)),
            pl.BlockSpec((r, SSM_HALO, CONV_DIM), lambda i: (i, 0, 0)),
            pl.BlockSpec((r, N_HEADS, HEADDIM, D_STATE), lambda i: (i, 0, 0, 0)),
            pl.BlockSpec((SSM_CONV, CONV_DIM), lambda i: (0, 0)),
            pl.BlockSpec((1, CONV_DIM), lambda i: (0, 0)),
            pl.BlockSpec((1, DT_PAD), lambda i: (0, 0)),
            pl.BlockSpec((1, D_INNER), lambda i: (0, 0)),
            pl.BlockSpec((1, D_INNER), lambda i: (0, 0)),
        ],
        out_specs=[
            pl.BlockSpec((r, steps, D_INNER), lambda i: (i, 0, 0)),
            pl.BlockSpec((r, SSM_HALO, CONV_DIM), lambda i: (i, 0, 0)),
            pl.BlockSpec((r, N_HEADS, HEADDIM, D_STATE), lambda i: (i, 0, 0, 0)),
        ],
        out_shape=[
            jax.ShapeDtypeStruct((nreq, steps, D_INNER), F32),
            jax.ShapeDtypeStruct((nreq, SSM_HALO, CONV_DIM), F32),
            jax.ShapeDtypeStruct((nreq, N_HEADS, HEADDIM, D_STATE), F32),
        ],
        scratch_shapes=[
            pltpu.VMEM((SSM_SLABS, SSD_SROWS, LANES), F32),
            pltpu.VMEM((SSD_ST, DT_PAD), F32),
            pltpu.VMEM((SSD_ST, D_INNER), F32),
        ],
        compiler_params=_params(1),
        name="ssd_sample",
    )(proj3, proj3, proj3, dt3, conv_state, h0, conv_w, conv_b, a_log, d_exp, norm_g)


def _tail_sample_kernel(x_ref, ca_ref, yn_ref, gate_ref, wpw_ref, bpw_ref, wssm_ref, wout_ref, gffn_ref,
                        wup_ref, wdown_ref, gfin_ref, y_ref):
    branch_a = _dot(ca_ref[...], wpw_ref[...]) + bpw_ref[...]
    branch_b = _dot(yn_ref[...].astype(BF16), wssm_ref[...])
    merged = (_sigmoid(gate_ref[:, 0:D_MODEL]) * branch_a
              + _sigmoid(gate_ref[:, D_MODEL:2 * D_MODEL]) * branch_b)
    h = x_ref[...] + _dot(merged.astype(BF16), wout_ref[...])
    y_ref[...] = _mlp_final(h, gffn_ref, wup_ref, wdown_ref, gfin_ref)


def _tail_sample(x2d, ca, yn, proj2d, w_pw, b_pw, w_ssm, w_out, g_ffn, w_up, w_down, g_final):
    m = x2d.shape[0]
    vec = _const_spec((1, D_MODEL))
    return pl.pallas_call(
        _tail_sample_kernel,
        grid=(1,),
        in_specs=[
            pl.BlockSpec((m, D_MODEL), lambda i: (0, 0)),
            pl.BlockSpec((m, C_CONV), lambda i: (0, 0)),
            pl.BlockSpec((m, D_INNER), lambda i: (0, 0)),
            pl.BlockSpec((m, 2 * D_MODEL), lambda i: (0, 1)),
            _const_spec((C_CONV, D_MODEL)), vec, _const_spec((D_INNER, D_MODEL)),
            _const_spec((D_MODEL, D_MODEL)), vec, _const_spec((D_MODEL, D_FF)),
            _const_spec((D_FF, D_MODEL)), vec,
        ],
        out_specs=pl.BlockSpec((m, D_MODEL), lambda i: (0, 0)),
        out_shape=jax.ShapeDtypeStruct((m, D_MODEL), F32),
        compiler_params=_params(1),
        name="tail_sample",
    )(x2d, ca, yn, proj2d, w_pw, b_pw, w_ssm, w_out, g_ffn, w_up, w_down, g_final)


def kernel(x_prompt, x_sample, state_conf_conv, state_ssm_conv, state_ssm, g_mix, w_in, conf_dw_w,
           conf_dw_b, conf_ln_g, conf_ln_b, conf_w_pw, conf_b_pw, ssm_conv_w, ssm_conv_b, ssm_dt_bias,
           ssm_a_log, ssm_d, ssm_norm_g, ssm_w_out, w_out, g_ffn, w_up, w_down, g_final):
    depth = w_in.shape[0]
    assert depth == 1
    nb, seq, _ = x_prompt.shape
    nreq, steps, _ = x_sample.shape
    i = 0

    def row(v):
        return v.reshape(1, -1)

    def pad_lanes(v, width):
        return jnp.pad(v, ((0, 0), (0, width - v.shape[1])))

    gm = row(g_mix[i])
    w_main = w_in[i].astype(BF16)
    w_dt = pad_lanes(w_main[:, MAIN_COLS:], DT_PAD)
    b_dt = pad_lanes(row(ssm_dt_bias[i]), DT_PAD)
    a_log = pad_lanes(row(ssm_a_log[i]), DT_PAD)
    d_exp = row(jnp.repeat(ssm_d[i], HEADDIM))
    norm_g = row(ssm_norm_g[i])
    dw_w = conf_dw_w[i]
    dw_w32 = jnp.pad(dw_w, ((0, 32 - CONF_KERNEL), (0, 0)))
    dw_b, ln_g, ln_b = row(conf_dw_b[i]), row(conf_ln_g[i]), row(conf_ln_b[i])
    cw, cb = ssm_conv_w[i], row(ssm_conv_b[i])
    w_pw, b_pw = conf_w_pw[i].astype(BF16), row(conf_b_pw[i])
    w_ssm = ssm_w_out[i].astype(BF16)
    mlp_w = (w_out[i].astype(BF16), row(g_ffn[i]), w_up[i].astype(BF16), w_down[i].astype(BF16),
             row(g_final))

    ga, conf_p = _conva_prompt(x_prompt, gm, w_main, dw_w, dw_b, ln_g, ln_b, w_pw, b_pw)
    head_row = jnp.arange(3 * DT_PAD, dtype=jnp.int32)[:, None] % DT_PAD
    ex = (head_row == jnp.arange(D_INNER, dtype=jnp.int32)[None, :] // HEADDIM).astype(BF16)
    gb, scv_p, h_p = _ssd_prompt(x_prompt, gm, w_main, w_dt, b_dt, w_ssm, cw, cb, a_log, d_exp, norm_g, ex)
    m_p = nb * seq
    y_p = _tail_prompt(x_prompt.reshape(m_p, D_MODEL), ga.reshape(m_p, D_MODEL), gb.reshape(m_p, D_MODEL),
                       *mlp_w)

    xs = x_sample.reshape(nreq * steps, D_MODEL)
    proj_s, dt_s = _inproj(xs, gm, w_main, w_dt, b_dt)
    proj_s3 = proj_s.reshape(nreq, steps, MAIN_COLS)
    ca_s, conf_s = _conva_sample(proj_s3, state_conf_conv[i], dw_w32, dw_b, ln_g, ln_b)
    yn_s, scv_s, h_s = _ssd_sample(proj_s3, dt_s.reshape(nreq, steps, DT_PAD), state_ssm_conv[i],
                                   state_ssm[i], cw, cb, a_log, d_exp, norm_g)
    y_s = _tail_sample(xs, ca_s, yn_s.reshape(nreq * steps, D_INNER), proj_s, w_pw, b_pw, w_ssm, *mlp_w)

    return (y_p.reshape(nb, seq, D_MODEL), y_s.reshape(nreq, steps, D_MODEL),
            conf_p[None], scv_p[None], h_p[None], conf_s[None], scv_s[None], h_s[None])
```

```python
import jax
import jax.numpy as jnp
from jax import lax
from jax.experimental import pallas as pl
from jax.experimental.pallas import tpu as pltpu

F32 = jnp.float32
BF16 = jnp.bfloat16

D_MODEL = 1024
C_CONV = 1024
CONF_KERNEL = 31
CONF_HALO = CONF_KERNEL - 1
D_INNER = 2048
HEADDIM = 64
N_HEADS = 32
N_GROUPS = 8
HEADS_PER_GROUP = 4
GROUP_W = HEADS_PER_GROUP * HEADDIM
D_STATE = 128
SSM_CONV = 4
SSM_HALO = SSM_CONV - 1
CONV_DIM = 4096
CHUNK = 128
D_FF = 4096
EPS = 1e-6
MAIN_COLS = 10240
DT_PAD = 128
LANES = 128
CONF_SLABS = C_CONV // LANES
SSM_SLABS = CONV_DIM // LANES
X_SLABS = D_INNER // LANES
VMEM_LIMIT = 56 * 1024 * 1024


def _sigmoid(x):
    return jax.nn.sigmoid(x)


def _silu(x):
    return x * jax.nn.sigmoid(x)


def _softplus(x):
    return jnp.maximum(x, 0.0) + jnp.log1p(jnp.exp(-jnp.abs(x)))


def _rmsnorm(x, g):
    return x * lax.rsqrt(jnp.mean(x * x, axis=-1, keepdims=True) + EPS) * g


def _dot(a, b):
    return jnp.dot(a, b, preferred_element_type=F32)


def _dot_nt(a, b):
    return lax.dot_general(a, b, (((1,), (1,)), ((), ())), preferred_element_type=F32)


def _dot_tn(a, b):
    return lax.dot_general(a, b, (((0,), (0,)), ((), ())), preferred_element_type=F32)


def _split3(x):
    hi = x.astype(BF16)
    r1 = x - hi.astype(F32)
    mid = r1.astype(BF16)
    lo = (r1 - mid.astype(F32)).astype(BF16)
    return hi, mid, lo


def _cumsum_rows(a, tri):
    hi, mid, lo = _split3(a)
    return _dot(tri, hi) + _dot(tri, mid) + _dot(tri, lo)


def _lanes(j):
    return slice(j * LANES, (j + 1) * LANES)


def _const_spec(shape, index=None):
    nd = len(shape)
    idx = index if index is not None else (0,) * nd
    return pl.BlockSpec(shape, lambda *_: idx, pipeline_mode=pl.Buffered(1))


def _params(ngrid):
    return pltpu.CompilerParams(dimension_semantics=("arbitrary",) * ngrid, vmem_limit_bytes=VMEM_LIMIT)


def _ln_swish(y, g, b):
    mu = jnp.mean(y, axis=-1, keepdims=True)
    yc = y - mu
    yn = yc * lax.rsqrt(jnp.mean(yc * yc, axis=-1, keepdims=True) + EPS) * g + b
    return _silu(yn)


def _mlp_final(h, gffn_ref, wup_ref, wdown_ref, gfin_ref):
    hidden = jnp.square(jnp.maximum(_dot(_rmsnorm(h, gffn_ref[...]).astype(BF16), wup_ref[...]), 0.0))
    h = h + _dot(hidden.astype(BF16), wdown_ref[...])
    return _rmsnorm(h, gfin_ref[...])


CONVA_T = 512
CONVA_RB = 64
CONVA_PAD = 32


def _conva_prompt_kernel(x_ref, gmix_ref, wa_ref, wb_ref, wga_ref, dw_ref, dwb_ref, lng_ref, lnb_ref,
                         wpw_ref, bpw_ref, ga_ref, st_ref, xe, conv, u_scr):
    t = CONVA_T

    @pl.when(pl.program_id(1) == 0)
    def _():
        xe[:, 0:CONVA_PAD, :] = jnp.zeros((CONF_SLABS, CONVA_PAD, LANES), F32)

    u_scr[...] = _rmsnorm(x_ref[0], gmix_ref[...]).astype(BF16)
    off = CONVA_PAD - CONF_HALO
    glu = _dot(u_scr[...], wa_ref[...]) * _sigmoid(_dot(u_scr[...], wb_ref[...]))
    for j in range(CONF_SLABS):
        xe[j, CONVA_PAD:CONVA_PAD + t, :] = glu[:, _lanes(j)]
    for j in range(CONF_SLABS):
        for rb in range(t // CONVA_RB):
            r0 = rb * CONVA_RB
            acc = jnp.broadcast_to(dwb_ref[:, _lanes(j)], (CONVA_RB, LANES))
            for k in range(CONF_KERNEL):
                acc = acc + dw_ref[k:k + 1, _lanes(j)] * xe[j, r0 + off + k:r0 + off + k + CONVA_RB, :]
            conv[r0:r0 + CONVA_RB, _lanes(j)] = acc
    ca = _ln_swish(conv[...], lng_ref[...], lnb_ref[...]).astype(BF16)
    ga_ref[0] = _sigmoid(_dot(u_scr[...], wga_ref[...])) * (_dot(ca, wpw_ref[...]) + bpw_ref[...])
    for j in range(CONF_SLABS):
        st_ref[0, :, _lanes(j)] = xe[j, t + off:t + CONVA_PAD, :]
        xe[j, 0:CONVA_PAD, :] = xe[j, t:t + CONVA_PAD, :]


def _conva_prompt(x3, g_mix, w_main, dw_w, dw_b, ln_g, ln_b, w_pw, b_pw):
    nb, seq, _ = x3.shape
    t = CONVA_T
    vec = _const_spec((1, C_CONV))
    sq = (D_MODEL, D_MODEL)
    return pl.pallas_call(
        _conva_prompt_kernel,
        grid=(nb, seq // t),
        in_specs=[
            pl.BlockSpec((1, t, D_MODEL), lambda b, i: (b, i, 0)),
            vec,
            _const_spec(sq, (0, 0)), _const_spec(sq, (0, 1)), _const_spec(sq, (0, 2)),
            _const_spec((CONF_KERNEL, C_CONV)), vec, vec, vec,
            _const_spec(sq), vec,
        ],
        out_specs=[
            pl.BlockSpec((1, t, D_MODEL), lambda b, i: (b, i, 0)),
            pl.BlockSpec((1, CONF_HALO, C_CONV), lambda b, i: (b, 0, 0)),
        ],
        out_shape=[
            jax.ShapeDtypeStruct((nb, seq, D_MODEL), F32),
            jax.ShapeDtypeStruct((nb, CONF_HALO, C_CONV), F32),
        ],
        scratch_shapes=[
            pltpu.VMEM((CONF_SLABS, t + CONVA_PAD, LANES), F32),
            pltpu.VMEM((t, C_CONV), F32),
            pltpu.VMEM((t, D_MODEL), BF16),
        ],
        compiler_params=_params(2),
        name="conva_prompt",
    )(x3, g_mix, w_main, w_main, w_main, dw_w, dw_b, ln_g, ln_b, w_pw, b_pw)


def _expand_heads(v, g, lo_half):
    rows = v.shape[0]

    def col(e):
        return jnp.broadcast_to(v[:, 4 * g + e:4 * g + e + 1], (rows, LANES))

    return jnp.concatenate([jnp.where(lo_half, col(0), col(1)), jnp.where(lo_half, col(2), col(3))], axis=1)


def _gated_group_norm(y, z, g):
    y = y * _silu(z)
    return y * lax.rsqrt(jnp.mean(y * y, axis=-1, keepdims=True) + EPS) * g


LOG2E = 1.4426950408889634


def _cat3(v):
    return jnp.concatenate(_split3(v), axis=1)


def _ssd_chunk(rows, dt, xs_scr, bc_scr, z_scr, yn_scr, h_ref, a_neg, dexp_ref, ng_ref, ex_ref, acumt_scr):
    t = CHUNK
    r_i = lax.broadcasted_iota(jnp.int32, (t, t), 0)
    c_i = lax.broadcasted_iota(jnp.int32, (t, t), 1)
    causal = r_i >= c_i
    tri = causal.astype(BF16)
    lo_half = lax.broadcasted_iota(jnp.int32, (t, LANES), 1) < HEADDIM
    zeros = jnp.zeros((t, LANES), F32)
    zeros_b = jnp.zeros((t, t), BF16)

    acum = _cumsum_rows(dt * a_neg, tri)
    a_last = acum[t - 1:t, :]
    ealast = jnp.exp(a_last)
    acum2 = acum * LOG2E
    acumt_scr[...] = acum2.T
    dt3 = _cat3(dt)

    def stage_a(g):
        gs = slice(g * GROUP_W, (g + 1) * GROUP_W)
        xs = xs_scr[rows, gs]
        bm = bc_scr[rows, _lanes(g)]
        cm = bc_scr[rows, _lanes(N_GROUPS + g)]
        scores = _dot_nt(cm, bm)
        h_g = h_ref[0, 4 * g:4 * g + 4].reshape(GROUP_W, D_STATE)
        colb = [jnp.broadcast_to(acum2[:, 4 * g + e:4 * g + e + 1], (t, LANES))
                for e in range(HEADS_PER_GROUP)]
        acx = jnp.concatenate([jnp.where(lo_half, colb[0], colb[1]),
                               jnp.where(lo_half, colb[2], colb[3])], axis=1)
        xdt = xs * _dot(dt3, ex_ref[:, gs])
        y = _dot_nt(cm, h_g.astype(BF16)) * jnp.exp2(acx) + xs * dexp_ref[:, gs]
        ms, xblk = [], []
        for e in range(HEADS_PER_GROUP):
            hd = 4 * g + e
            seg = colb[e] - acumt_scr[hd:hd + 1, :]
            m = (scores * jnp.exp2(seg)).astype(BF16)
            ms.append(jnp.where(causal, m, zeros_b))
            half = xdt[:, _lanes(e // 2)]
            keep = jnp.where(lo_half, half, 0.0) if e % 2 == 0 else jnp.where(lo_half, 0.0, half)
            blk = jnp.concatenate([keep, zeros] if e < 2 else [zeros, keep], axis=1)
            xblk.append(blk.astype(BF16))
        xw = (xdt * jnp.exp2(acx[t - 1:t, :] - acx)).astype(BF16)
        return y, jnp.concatenate(ms, axis=1), jnp.concatenate(xblk, axis=0), xw, bm

    def stage_b(g, staged):
        y, mcat, xcat, xw, bm = staged
        gs = slice(g * GROUP_W, (g + 1) * GROUP_W)
        y = y + _dot(mcat, xcat)
        upd = _dot_tn(xw, bm)
        for e in range(HEADS_PER_GROUP):
            hd = 4 * g + e
            h_ref[0, hd] = h_ref[0, hd] * ealast[:, hd:hd + 1] + upd[e * HEADDIM:(e + 1) * HEADDIM, :]
        yn_scr[rows, gs] = _gated_group_norm(y, z_scr[rows, gs], ng_ref[:, gs]).astype(BF16)

    staged = stage_a(0)
    for g in range(N_GROUPS):
        following = stage_a(g + 1) if g + 1 < N_GROUPS else None
        stage_b(g, staged)
        staged = following


SSDP_T = 256
SSDP_RB = 64
SSD_PAD = 8


def _ssd_prompt_kernel(x_ref, gmix_ref, wz_ref, wx_ref, wbc_ref, wdt_ref, bdt_ref, wgb_ref, wssm_ref,
                       cw_ref, cb_ref, alog_ref, dexp_ref, ng_ref, ex_ref,
                       gb_ref, ncv_ref, h_ref,
                       xe, u_scr, xs_scr, bc_scr, z_scr, yn_scr, acumt_scr):
    t = SSDP_T

    @pl.when(pl.program_id(1) == 0)
    def _():
        xe[:, 0:SSD_PAD, :] = jnp.zeros((SSM_SLABS, SSD_PAD, LANES), F32)
        h_ref[...] = jnp.zeros(h_ref.shape, F32)

    u_scr[...] = _rmsnorm(x_ref[0], gmix_ref[...]).astype(BF16)
    a_neg = -jnp.exp(alog_ref[...])
    off = SSD_PAD - SSM_HALO
    for c in range(t // CHUNK):
        c0 = c * CHUNK
        rows = slice(c0, c0 + CHUNK)
        u_c = u_scr[rows, :]
        z_scr[rows, :] = _dot(u_c, wz_ref[...])
        dt = _softplus(_dot(u_c, wdt_ref[...]) + bdt_ref[...])
        xr = _dot(u_c, wx_ref[...])
        for j in range(X_SLABS):
            xe[j, SSD_PAD + c0:SSD_PAD + c0 + CHUNK, :] = xr[:, _lanes(j)]
        bcr = _dot(u_c, wbc_ref[...])
        for j in range(X_SLABS):
            xe[X_SLABS + j, SSD_PAD + c0:SSD_PAD + c0 + CHUNK, :] = bcr[:, _lanes(j)]
        for j in range(SSM_SLABS):
            for rb in range(CHUNK // SSDP_RB):
                r0 = c0 + rb * SSDP_RB
                acc = jnp.broadcast_to(cb_ref[:, _lanes(j)], (SSDP_RB, LANES))
                for k in range(SSM_CONV):
                    acc = acc + cw_ref[k:k + 1, _lanes(j)] * xe[j, r0 + off + k:r0 + off + k + SSDP_RB, :]
                if j < X_SLABS:
                    xs_scr[r0:r0 + SSDP_RB, _lanes(j)] = _silu(acc)
                else:
                    bc_scr[r0:r0 + SSDP_RB, _lanes(j - X_SLABS)] = _silu(acc).astype(BF16)
        _ssd_chunk(rows, dt, xs_scr, bc_scr, z_scr, yn_scr, h_ref, a_neg, dexp_ref, ng_ref, ex_ref, acumt_scr)
        gb_ref[0, rows, :] = _sigmoid(_dot(u_c, wgb_ref[...])) * _dot(yn_scr[rows, :], wssm_ref[...])
    for j in range(SSM_SLABS):
        ncv_ref[0, :, _lanes(j)] = xe[j, t + off:t + SSD_PAD, :]
        xe[j, 0:SSD_PAD, :] = xe[j, t:t + SSD_PAD, :]


def _ssd_prompt(x3, g_mix, w_main, w_dt, b_dt, w_ssm, conv_w, conv_b, a_log, d_exp, norm_g, ex):
    nb, seq, _ = x3.shape
    t = SSDP_T
    wide = (D_MODEL, D_INNER)
    return pl.pallas_call(
        _ssd_prompt_kernel,
        grid=(nb, seq // t),
        in_specs=[
            pl.BlockSpec((1, t, D_MODEL), lambda b, i: (b, i, 0)),
            _const_spec((1, D_MODEL)),
            _const_spec(wide, (0, 2)), _const_spec(wide, (0, 3)), _const_spec(wide, (0, 4)),
            _const_spec((D_MODEL, DT_PAD)), _const_spec((1, DT_PAD)),
            _const_spec((D_MODEL, D_MODEL), (0, 3)),
            _const_spec((D_INNER, D_MODEL)),
            _const_spec((SSM_CONV, CONV_DIM)), _const_spec((1, CONV_DIM)), _const_spec((1, DT_PAD)),
            _const_spec((1, D_INNER)), _const_spec((1, D_INNER)),
            _const_spec((3 * DT_PAD, D_INNER)),
        ],
        out_specs=[
            pl.BlockSpec((1, t, D_MODEL), lambda b, i: (b, i, 0)),
            pl.BlockSpec((1, SSM_HALO, CONV_DIM), lambda b, i: (b, 0, 0)),
            pl.BlockSpec((1, N_HEADS, HEADDIM, D_STATE), lambda b, i: (b, 0, 0, 0)),
        ],
        out_shape=[
            jax.ShapeDtypeStruct((nb, seq, D_MODEL), F32),
            jax.ShapeDtypeStruct((nb, SSM_HALO, CONV_DIM), F32),
            jax.ShapeDtypeStruct((nb, N_HEADS, HEADDIM, D_STATE), F32),
        ],
        scratch_shapes=[
            pltpu.VMEM((SSM_SLABS, t + SSD_PAD, LANES), F32),
            pltpu.VMEM((t, D_MODEL), BF16),
            pltpu.VMEM((t, D_INNER), F32),
            pltpu.VMEM((t, D_INNER), BF16),
            pltpu.VMEM((t, D_INNER), F32),
            pltpu.VMEM((t, D_INNER), BF16),
            pltpu.VMEM((DT_PAD, CHUNK), F32),
        ],
        compiler_params=_params(2),
        name="ssd_prompt",
    )(x3, g_mix, w_main, w_main, w_main, w_dt, b_dt, w_main, w_ssm, conv_w, conv_b, a_log, d_exp, norm_g,
      ex)


def _tail_prompt_kernel(x_ref, ga_ref, gb_ref, wout_ref, gffn_ref, wup_ref, wdown_ref, gfin_ref, y_ref):
    merged = (ga_ref[...] + gb_ref[...]).astype(BF16)
    h = x_ref[...] + _dot(merged, wout_ref[...])
    y_ref[...] = _mlp_final(h, gffn_ref, wup_ref, wdown_ref, gfin_ref)


def _tail_prompt(x2d, ga, gb, w_out, g_ffn, w_up, w_down, g_final, tm=512):
    m = x2d.shape[0]
    tile = pl.BlockSpec((tm, D_MODEL), lambda i: (i, 0))
    vec = _const_spec((1, D_MODEL))
    return pl.pallas_call(
        _tail_prompt_kernel,
        grid=(m // tm,),
        in_specs=[tile, tile, tile, _const_spec((D_MODEL, D_MODEL)), vec,
                  _const_spec((D_MODEL, D_FF)), _const_spec((D_FF, D_MODEL)), vec],
        out_specs=tile,
        out_shape=jax.ShapeDtypeStruct((m, D_MODEL), F32),
        compiler_params=_params(1),
        name="tail_prompt",
    )(x2d, ga, gb, w_out, g_ffn, w_up, w_down, g_final)


def _inproj_kernel(x_ref, g_ref, w_ref, wdt_ref, bdt_ref, proj_ref, dt_ref, u_scr):
    @pl.when(pl.program_id(0) == 0)
    def _():
        ub = _rmsnorm(x_ref[...], g_ref[...]).astype(BF16)
        u_scr[...] = ub
        dt_ref[...] = _softplus(_dot(ub, wdt_ref[...]) + bdt_ref[...])

    proj_ref[...] = _dot(u_scr[...], w_ref[...])


def _inproj(x2d, g_mix, w_main, w_dt, b_dt, tn=1024):
    m = x2d.shape[0]
    return pl.pallas_call(
        _inproj_kernel,
        grid=(MAIN_COLS // tn,),
        in_specs=[
            pl.BlockSpec((m, D_MODEL), lambda j: (0, 0)),
            pl.BlockSpec((1, D_MODEL), lambda j: (0, 0)),
            pl.BlockSpec((D_MODEL, tn), lambda j: (0, j)),
            pl.BlockSpec((D_MODEL, DT_PAD), lambda j: (0, 0)),
            pl.BlockSpec((1, DT_PAD), lambda j: (0, 0)),
        ],
        out_specs=[
            pl.BlockSpec((m, tn), lambda j: (0, j)),
            pl.BlockSpec((m, DT_PAD), lambda j: (0, 0)),
        ],
        out_shape=[
            jax.ShapeDtypeStruct((m, MAIN_COLS), F32),
            jax.ShapeDtypeStruct((m, DT_PAD), F32),
        ],
        scratch_shapes=[pltpu.VMEM((m, D_MODEL), BF16)],
        compiler_params=_params(1),
        name="inproj_sample",
    )(x2d, g_mix, w_main, w_dt, b_dt)


CONVA_SR = 32
CONVA_SROWS = 40


def _conva_sample_kernel(a_ref, b_ref, st_ref, w_ref, bias_ref, lng_ref, lnb_ref,
                         ca_ref, nst_ref, xe, conv):
    nreq = st_ref.shape[1]
    steps = a_ref.shape[0] // nreq
    xe[:, 32:CONVA_SROWS, :] = jnp.zeros((CONF_SLABS, CONVA_SROWS - 32, LANES), F32)
    per = 8 // steps

    def body(p, carry):
        r8 = pl.ds(pl.multiple_of(p * 8, 8), 8)
        glu8 = a_ref[r8, :] * _sigmoid(b_ref[r8, :])
        rows = []
        for q in range(per):
            r = p * per + q
            st = st_ref[0, r]
            glu = glu8[q * steps:(q + 1) * steps, :]
            for j in range(CONF_SLABS):
                xe[j, 0:CONF_HALO, :] = st[:, _lanes(j)]
                xe[j, CONF_HALO:CONF_HALO + steps, :] = glu[:, _lanes(j)]
                nst_ref[0, r, :, _lanes(j)] = xe[j, steps:steps + CONF_HALO, :]
            for t in range(steps):
                rows.append(jnp.concatenate(
                    [jnp.sum(w_ref[:, _lanes(j)] * xe[j, t:t + 32, :], axis=0, keepdims=True)
                     for j in range(CONF_SLABS)], axis=1))
        conv[r8, :] = jnp.concatenate(rows, axis=0) + bias_ref[...]
        return carry

    lax.fori_loop(0, nreq // per, body, 0)
    ca_ref[...] = _ln_swish(conv[...], lng_ref[...], lnb_ref[...]).astype(BF16)


def _conva_sample(proj2d, state, steps, dw_w32, dw_b, ln_g, ln_b):
    depth, nreq = state.shape[0], state.shape[1]
    r = CONVA_SR
    vec = pl.BlockSpec((1, C_CONV), lambda i: (0, 0))
    st_spec = pl.BlockSpec((depth, r, CONF_HALO, C_CONV), lambda i: (0, i, 0, 0))
    return pl.pallas_call(
        _conva_sample_kernel,
        grid=(nreq // r,),
        in_specs=[
            pl.BlockSpec((r * steps, C_CONV), lambda i: (i, 0)),
            pl.BlockSpec((r * steps, C_CONV), lambda i: (i, 1)),
            st_spec,
            pl.BlockSpec((32, C_CONV), lambda i: (0, 0)),
            vec, vec, vec,
        ],
        out_specs=[pl.BlockSpec((r * steps, C_CONV), lambda i: (i, 0)), st_spec],
        out_shape=[
            jax.ShapeDtypeStruct((nreq * steps, C_CONV), BF16),
            jax.ShapeDtypeStruct(state.shape, F32),
        ],
        scratch_shapes=[pltpu.VMEM((CONF_SLABS, CONVA_SROWS, LANES), F32),
                        pltpu.VMEM((r * steps, C_CONV), F32)],
        compiler_params=_params(1),
        name="conva_sample",
    )(proj2d, proj2d, state, dw_w32, dw_b, ln_g, ln_b)


SSD_SR = 4
SSD_ST = 8
SSD_SROWS = 16


def _ssd_sample_kernel(z_ref, xr_ref, bcr_ref, dt_ref, cst_ref, h0_ref, cw_ref, cb_ref, alog_ref,
                       dexp_ref, ng_ref, yn_ref, ncv_ref, h_ref, xe, dt8, z8):
    nreq = cst_ref.shape[1]
    steps = z_ref.shape[0] // nreq
    t = SSD_ST
    xe[...] = jnp.zeros(xe.shape, F32)
    dt8[...] = jnp.zeros(dt8.shape, F32)
    z8[...] = jnp.zeros(z8.shape, F32)
    rows = lax.broadcasted_iota(jnp.int32, (t, 1), 0)
    live = rows < steps
    lo_half = lax.broadcasted_iota(jnp.int32, (t, LANES), 1) < HEADDIM
    a_neg = -jnp.exp(alog_ref[...])
    groups = range(N_GROUPS)

    def conv(j):
        acc = jnp.broadcast_to(cb_ref[:, _lanes(j)], (t, LANES))
        for k in range(SSM_CONV):
            acc = acc + cw_ref[k:k + 1, _lanes(j)] * xe[j, k:k + t, :]
        return jnp.where(live, _silu(acc), 0.0)

    for r in range(nreq):
        rr = slice(r * steps, (r + 1) * steps)
        cst = cst_ref[0, r]
        xr = xr_ref[rr, :]
        bcr = bcr_ref[rr, :]
        for j in range(SSM_SLABS):
            xe[j, 0:SSM_HALO, :] = cst[:, _lanes(j)]
            new = xr[:, _lanes(j)] if j < X_SLABS else bcr[:, _lanes(j - X_SLABS)]
            xe[j, SSM_HALO:SSM_HALO + steps, :] = new
            ncv_ref[0, r, :, _lanes(j)] = xe[j, steps:steps + SSM_HALO, :]
        dt8[0:steps, :] = dt_ref[rr, :]
        z8[0:steps, :] = z_ref[rr, :]

        dt = dt8[...]
        a = dt * a_neg
        acum = jnp.zeros((t, DT_PAD), F32)
        for s in range(steps):
            acum = acum + jnp.where(rows >= s, a[s:s + 1, :], 0.0)
        ealast = jnp.exp(acum[t - 1:t, :])

        tiles = [conv(j) for j in range(SSM_SLABS)]
        xs = [jnp.concatenate([tiles[2 * g], tiles[2 * g + 1]], axis=1) for g in groups]
        bm = [tiles[X_SLABS + g] for g in groups]
        cm = [tiles[X_SLABS + N_GROUPS + g] for g in groups]
        scores = [_dot_nt(cm[g], bm[g]) for g in groups]
        yoff = [_dot_nt(cm[g], h0_ref[0, r, 4 * g:4 * g + 4].reshape(GROUP_W, D_STATE)) for g in groups]
        acx = [_expand_heads(acum, g, lo_half) for g in groups]
        xdt = [xs[g] * _expand_heads(dt, g, lo_half) for g in groups]
        ys = []
        for g in groups:
            gs = slice(g * GROUP_W, (g + 1) * GROUP_W)
            y = yoff[g] * jnp.exp(acx[g]) + xs[g] * dexp_ref[:, gs]
            for s in range(steps):
                decay = jnp.exp(jnp.where(rows >= s, acx[g] - acx[g][s:s + 1, :], -jnp.inf))
                y = y + (scores[g][:, s:s + 1] * decay) * xdt[g][s:s + 1, :]
            ys.append(y)
        upd = [_dot_tn(xdt[g] * jnp.exp(acx[g][t - 1:t, :] - acx[g]), bm[g]) for g in groups]
        for g in groups:
            for e in range(HEADS_PER_GROUP):
                hd = 4 * g + e
                h_ref[0, r, hd] = (h0_ref[0, r, hd] * ealast[:, hd:hd + 1]
                                   + upd[g][e * HEADDIM:(e + 1) * HEADDIM, :])
        for g in groups:
            gs = slice(g * GROUP_W, (g + 1) * GROUP_W)
            yn = _gated_group_norm(ys[g], z8[:, gs], ng_ref[:, gs])
            yn_ref[rr, gs] = yn[0:steps, :]


def _ssd_sample(proj2d, dt2d, conv_state, h0, steps, conv_w, conv_b, a_log, d_exp, norm_g):
    depth, nreq = conv_state.shape[0], conv_state.shape[1]
    r = SSD_SR
    cst_spec = pl.BlockSpec((depth, r, SSM_HALO, CONV_DIM), lambda i: (0, i, 0, 0))
    h_spec = pl.BlockSpec((depth, r, N_HEADS, HEADDIM, D_STATE), lambda i: (0, i, 0, 0, 0))
    return pl.pallas_call(
        _ssd_sample_kernel,
        grid=(nreq // r,),
        in_specs=[
            pl.BlockSpec((r * steps, D_INNER), lambda i: (i, 2)),
            pl.BlockSpec((r * steps, D_INNER), lambda i: (i, 3)),
            pl.BlockSpec((r * steps, D_INNER), lambda i: (i, 4)),
            pl.BlockSpec((r * steps, DT_PAD), lambda i: (i, 0)),
            cst_spec,
            h_spec,
            pl.BlockSpec((SSM_CONV, CONV_DIM), lambda i: (0, 0)),
            pl.BlockSpec((1, CONV_DIM), lambda i: (0, 0)),
            pl.BlockSpec((1, DT_PAD), lambda i: (0, 0)),
            pl.BlockSpec((1, D_INNER), lambda i: (0, 0)),
            pl.BlockSpec((1, D_INNER), lambda i: (0, 0)),
        ],
        out_specs=[pl.BlockSpec((r * steps, D_INNER), lambda i: (i, 0)), cst_spec, h_spec],
        out_shape=[
            jax.ShapeDtypeStruct((nreq * steps, D_INNER), F32),
            jax.ShapeDtypeStruct(conv_state.shape, F32),
            jax.ShapeDtypeStruct(h0.shape, F32),
        ],
        scratch_shapes=[
            pltpu.VMEM((SSM_SLABS, SSD_SROWS, LANES), F32),
            pltpu.VMEM((SSD_ST, DT_PAD), F32),
            pltpu.VMEM((SSD_ST, D_INNER), F32),
        ],
        compiler_params=_params(1),
        name="ssd_sample",
    )(proj2d, proj2d, proj2d, dt2d, conv_state, h0, conv_w, conv_b, a_log, d_exp, norm_g)


def _tail_sample_kernel(x_ref, ca_ref, yn_ref, gate_ref, wpw_ref, bpw_ref, wssm_ref, wout_ref, gffn_ref,
                        wup_ref, wdown_ref, gfin_ref, y_ref):
    branch_a = _dot(ca_ref[...], wpw_ref[...]) + bpw_ref[...]
    branch_b = _dot(yn_ref[...].astype(BF16), wssm_ref[...])
    merged = (_sigmoid(gate_ref[:, 0:D_MODEL]) * branch_a
              + _sigmoid(gate_ref[:, D_MODEL:2 * D_MODEL]) * branch_b)
    h = x_ref[...] + _dot(merged.astype(BF16), wout_ref[...])
    y_ref[...] = _mlp_final(h, gffn_ref, wup_ref, wdown_ref, gfin_ref)


def _tail_sample(x2d, ca, yn, proj2d, w_pw, b_pw, w_ssm, w_out, g_ffn, w_up, w_down, g_final):
    m = x2d.shape[0]
    vec = _const_spec((1, D_MODEL))
    return pl.pallas_call(
        _tail_sample_kernel,
        grid=(1,),
        in_specs=[
            pl.BlockSpec((m, D_MODEL), lambda i: (0, 0)),
            pl.BlockSpec((m, C_CONV), lambda i: (0, 0)),
            pl.BlockSpec((m, D_INNER), lambda i: (0, 0)),
            pl.BlockSpec((m, 2 * D_MODEL), lambda i: (0, 1)),
            _const_spec((C_CONV, D_MODEL)), vec, _const_spec((D_INNER, D_MODEL)),
            _const_spec((D_MODEL, D_MODEL)), vec, _const_spec((D_MODEL, D_FF)),
            _const_spec((D_FF, D_MODEL)), vec,
        ],
        out_specs=pl.BlockSpec((m, D_MODEL), lambda i: (0, 0)),
        out_shape=jax.ShapeDtypeStruct((m, D_MODEL), F32),
        compiler_params=_params(1),
        name="tail_sample",
    )(x2d, ca, yn, proj2d, w_pw, b_pw, w_ssm, w_out, g_ffn, w_up, w_down, g_final)


def kernel(x_prompt, x_sample, state_conf_conv, state_ssm_conv, state_ssm, g_mix, w_in, conf_dw_w,
           conf_dw_b, conf_ln_g, conf_ln_b, conf_w_pw, conf_b_pw, ssm_conv_w, ssm_conv_b, ssm_dt_bias,
           ssm_a_log, ssm_d, ssm_norm_g, ssm_w_out, w_out, g_ffn, w_up, w_down, g_final):
    depth = w_in.shape[0]
    assert depth == 1
    nb, seq, _ = x_prompt.shape
    nreq, steps, _ = x_sample.shape
    i = 0

    def row(v):
        return v.reshape(1, -1)

    def pad_lanes(v, width):
        return jnp.pad(v, ((0, 0), (0, width - v.shape[1])))

    gm = row(g_mix[i])
    w_main = w_in[i].astype(BF16)
    w_dt = pad_lanes(w_main[:, MAIN_COLS:], DT_PAD)
    b_dt = pad_lanes(row(ssm_dt_bias[i]), DT_PAD)
    a_log = pad_lanes(row(ssm_a_log[i]), DT_PAD)
    d_exp = row(jnp.repeat(ssm_d[i], HEADDIM))
    norm_g = row(ssm_norm_g[i])
    dw_w = conf_dw_w[i]
    dw_w32 = jnp.pad(dw_w, ((0, 32 - CONF_KERNEL), (0, 0)))
    dw_b, ln_g, ln_b = row(conf_dw_b[i]), row(conf_ln_g[i]), row(conf_ln_b[i])
    cw, cb = ssm_conv_w[i], row(ssm_conv_b[i])
    w_pw, b_pw = conf_w_pw[i].astype(BF16), row(conf_b_pw[i])
    w_ssm = ssm_w_out[i].astype(BF16)
    mlp_w = (w_out[i].astype(BF16), row(g_ffn[i]), w_up[i].astype(BF16), w_down[i].astype(BF16),
             row(g_final))

    ga, conf_p = _conva_prompt(x_prompt, gm, w_main, dw_w, dw_b, ln_g, ln_b, w_pw, b_pw)
    head_row = jnp.arange(3 * DT_PAD, dtype=jnp.int32)[:, None] % DT_PAD
    ex = (head_row == jnp.arange(D_INNER, dtype=jnp.int32)[None, :] // HEADDIM).astype(BF16)
    gb, scv_p, h_p = _ssd_prompt(x_prompt, gm, w_main, w_dt, b_dt, w_ssm, cw, cb, a_log, d_exp, norm_g, ex)
    m_p = nb * seq
    y_p = _tail_prompt(x_prompt.reshape(m_p, D_MODEL), ga.reshape(m_p, D_MODEL), gb.reshape(m_p, D_MODEL),
                       *mlp_w)

    xs = x_sample.reshape(nreq * steps, D_MODEL)
    proj_s, dt_s = _inproj(xs, gm, w_main, w_dt, b_dt)
    ca_s, conf_s = _conva_sample(proj_s, state_conf_conv, steps, dw_w32, dw_b, ln_g, ln_b)
    yn_s, scv_s, h_s = _ssd_sample(proj_s, dt_s, state_ssm_conv, state_ssm, steps, cw, cb, a_log, d_exp, norm_g)
    y_s = _tail_sample(xs, ca_s, yn_s, proj_s, w_pw, b_pw, w_ssm, *mlp_w)

    return (y_p.reshape(nb, seq, D_MODEL), y_s.reshape(nreq, steps, D_MODEL),
            conf_p[None], scv_p[None], h_p[None], conf_s, scv_s, h_s)
```

```python
import jax
import jax.numpy as jnp
from jax import lax
from jax.experimental import pallas as pl
from jax.experimental.pallas import tpu as pltpu

F32 = jnp.float32
BF16 = jnp.bfloat16

D_MODEL = 1024
C_CONV = 1024
CONF_KERNEL = 31
CONF_HALO = CONF_KERNEL - 1
D_INNER = 2048
HEADDIM = 64
N_HEADS = 32
N_GROUPS = 8
HEADS_PER_GROUP = 4
GROUP_W = HEADS_PER_GROUP * HEADDIM
D_STATE = 128
SSM_CONV = 4
SSM_HALO = SSM_CONV - 1
CONV_DIM = 4096
CHUNK = 128
D_FF = 4096
EPS = 1e-6
MAIN_COLS = 10240
DT_PAD = 128
LANES = 128
CONF_SLABS = C_CONV // LANES
SSM_SLABS = CONV_DIM // LANES
X_SLABS = D_INNER // LANES
VMEM_LIMIT = 56 * 1024 * 1024


def _sigmoid(x):
    return jax.nn.sigmoid(x)


def _silu(x):
    return x * jax.nn.sigmoid(x)


def _softplus(x):
    return jnp.maximum(x, 0.0) + jnp.log1p(jnp.exp(-jnp.abs(x)))


def _rmsnorm(x, g):
    return x * lax.rsqrt(jnp.mean(x * x, axis=-1, keepdims=True) + EPS) * g


def _dot(a, b):
    return jnp.dot(a, b, preferred_element_type=F32)


def _dot_nt(a, b):
    return lax.dot_general(a, b, (((1,), (1,)), ((), ())), preferred_element_type=F32)


def _dot_tn(a, b):
    return lax.dot_general(a, b, (((0,), (0,)), ((), ())), preferred_element_type=F32)


def _split3(x):
    hi = x.astype(BF16)
    r1 = x - hi.astype(F32)
    mid = r1.astype(BF16)
    lo = (r1 - mid.astype(F32)).astype(BF16)
    return hi, mid, lo


def _cumsum_rows(a, tri):
    hi, mid, lo = _split3(a)
    return _dot(tri, hi) + _dot(tri, mid) + _dot(tri, lo)


def _lanes(j):
    return slice(j * LANES, (j + 1) * LANES)


def _const_spec(shape, index=None):
    nd = len(shape)
    idx = index if index is not None else (0,) * nd
    return pl.BlockSpec(shape, lambda *_: idx, pipeline_mode=pl.Buffered(1))


def _params(ngrid):
    return pltpu.CompilerParams(dimension_semantics=("arbitrary",) * ngrid, vmem_limit_bytes=VMEM_LIMIT)


def _ln_swish(y, g, b):
    mu = jnp.mean(y, axis=-1, keepdims=True)
    yc = y - mu
    yn = yc * lax.rsqrt(jnp.mean(yc * yc, axis=-1, keepdims=True) + EPS) * g + b
    return _silu(yn)


def _mlp_final(h, gffn_ref, wup_ref, wdown_ref, gfin_ref):
    hidden = jnp.square(jnp.maximum(_dot(_rmsnorm(h, gffn_ref[...]).astype(BF16), wup_ref[...]), 0.0))
    h = h + _dot(hidden.astype(BF16), wdown_ref[...])
    return _rmsnorm(h, gfin_ref[...])


CONVA_T = 512
CONVA_RB = 64
CONVA_PAD = 32


def _conva_prompt_kernel(x_ref, gmix_ref, wa_ref, wb_ref, wga_ref, dw_ref, dwb_ref, lng_ref, lnb_ref,
                         wpw_ref, bpw_ref, ga_ref, st_ref, xe, conv, u_scr):
    t = CONVA_T

    @pl.when(pl.program_id(1) == 0)
    def _():
        xe[:, 0:CONVA_PAD, :] = jnp.zeros((CONF_SLABS, CONVA_PAD, LANES), F32)

    u_scr[...] = _rmsnorm(x_ref[0], gmix_ref[...]).astype(BF16)
    off = CONVA_PAD - CONF_HALO
    glu = _dot(u_scr[...], wa_ref[...]) * _sigmoid(_dot(u_scr[...], wb_ref[...]))
    for j in range(CONF_SLABS):
        xe[j, CONVA_PAD:CONVA_PAD + t, :] = glu[:, _lanes(j)]
    for j in range(CONF_SLABS):
        for rb in range(t // CONVA_RB):
            r0 = rb * CONVA_RB
            acc = jnp.broadcast_to(dwb_ref[:, _lanes(j)], (CONVA_RB, LANES))
            for k in range(CONF_KERNEL):
                acc = acc + dw_ref[k:k + 1, _lanes(j)] * xe[j, r0 + off + k:r0 + off + k + CONVA_RB, :]
            conv[r0:r0 + CONVA_RB, _lanes(j)] = acc
    ca = _ln_swish(conv[...], lng_ref[...], lnb_ref[...]).astype(BF16)
    ga_ref[0] = _sigmoid(_dot(u_scr[...], wga_ref[...])) * (_dot(ca, wpw_ref[...]) + bpw_ref[...])
    for j in range(CONF_SLABS):
        st_ref[0, :, _lanes(j)] = xe[j, t + off:t + CONVA_PAD, :]
        xe[j, 0:CONVA_PAD, :] = xe[j, t:t + CONVA_PAD, :]


def _conva_prompt(x3, g_mix, w_main, dw_w, dw_b, ln_g, ln_b, w_pw, b_pw):
    nb, seq, _ = x3.shape
    t = CONVA_T
    vec = _const_spec((1, C_CONV))
    sq = (D_MODEL, D_MODEL)
    return pl.pallas_call(
        _conva_prompt_kernel,
        grid=(nb, seq // t),
        in_specs=[
            pl.BlockSpec((1, t, D_MODEL), lambda b, i: (b, i, 0)),
            vec,
            _const_spec(sq, (0, 0)), _const_spec(sq, (0, 1)), _const_spec(sq, (0, 2)),
            _const_spec((CONF_KERNEL, C_CONV)), vec, vec, vec,
            _const_spec(sq), vec,
        ],
        out_specs=[
            pl.BlockSpec((1, t, D_MODEL), lambda b, i: (b, i, 0)),
            pl.BlockSpec((1, CONF_HALO, C_CONV), lambda b, i: (b, 0, 0)),
        ],
        out_shape=[
            jax.ShapeDtypeStruct((nb, seq, D_MODEL), F32),
            jax.ShapeDtypeStruct((nb, CONF_HALO, C_CONV), F32),
        ],
        scratch_shapes=[
            pltpu.VMEM((CONF_SLABS, t + CONVA_PAD, LANES), F32),
            pltpu.VMEM((t, C_CONV), F32),
            pltpu.VMEM((t, D_MODEL), BF16),
        ],
        compiler_params=_params(2),
        name="conva_prompt",
    )(x3, g_mix, w_main, w_main, w_main, dw_w, dw_b, ln_g, ln_b, w_pw, b_pw)


def _expand_heads(v, g, lo_half):
    rows = v.shape[0]

    def col(e):
        return jnp.broadcast_to(v[:, 4 * g + e:4 * g + e + 1], (rows, LANES))

    return jnp.concatenate([jnp.where(lo_half, col(0), col(1)), jnp.where(lo_half, col(2), col(3))], axis=1)


def _gated_group_norm(y, z, g):
    y = y * _silu(z)
    return y * lax.rsqrt(jnp.mean(y * y, axis=-1, keepdims=True) + EPS) * g


LOG2E = 1.4426950408889634


def _cat3(v):
    return jnp.concatenate(_split3(v), axis=1)


def _ssd_chunk(rows, dt, xs_scr, bc_scr, z_scr, yn_scr, h_ref, a_neg, dexp_ref, ng_ref, ex_ref, acumt_scr):
    t = CHUNK
    r_i = lax.broadcasted_iota(jnp.int32, (t, t), 0)
    c_i = lax.broadcasted_iota(jnp.int32, (t, t), 1)
    causal = r_i >= c_i
    tri = causal.astype(BF16)
    lo_half = lax.broadcasted_iota(jnp.int32, (t, LANES), 1) < HEADDIM
    zeros = jnp.zeros((t, LANES), F32)
    zeros_b = jnp.zeros((t, t), BF16)

    acum = _cumsum_rows(dt * a_neg, tri)
    a_last = acum[t - 1:t, :]
    ealast = jnp.exp(a_last)
    acum2 = acum * LOG2E
    acumt_scr[...] = acum2.T
    dt3 = _cat3(dt)

    def stage_a(g):
        gs = slice(g * GROUP_W, (g + 1) * GROUP_W)
        xs = xs_scr[rows, gs]
        bm = bc_scr[rows, _lanes(g)]
        cm = bc_scr[rows, _lanes(N_GROUPS + g)]
        scores = _dot_nt(cm, bm)
        h_g = h_ref[0, 4 * g:4 * g + 4].reshape(GROUP_W, D_STATE)
        colb = [jnp.broadcast_to(acum2[:, 4 * g + e:4 * g + e + 1], (t, LANES))
                for e in range(HEADS_PER_GROUP)]
        acx = jnp.concatenate([jnp.where(lo_half, colb[0], colb[1]),
                               jnp.where(lo_half, colb[2], colb[3])], axis=1)
        xdt = xs * _dot(dt3, ex_ref[:, gs])
        y = _dot_nt(cm, h_g.astype(BF16)) * jnp.exp2(acx) + xs * dexp_ref[:, gs]
        ms, xblk = [], []
        for e in range(HEADS_PER_GROUP):
            hd = 4 * g + e
            seg = colb[e] - acumt_scr[hd:hd + 1, :]
            m = (scores * jnp.exp2(seg)).astype(BF16)
            ms.append(jnp.where(causal, m, zeros_b))
            half = xdt[:, _lanes(e // 2)]
            keep = jnp.where(lo_half, half, 0.0) if e % 2 == 0 else jnp.where(lo_half, 0.0, half)
            blk = jnp.concatenate([keep, zeros] if e < 2 else [zeros, keep], axis=1)
            xblk.append(blk.astype(BF16))
        xw = (xdt * jnp.exp2(acx[t - 1:t, :] - acx)).astype(BF16)
        return y, jnp.concatenate(ms, axis=1), jnp.concatenate(xblk, axis=0), xw, bm

    def stage_b(g, staged):
        y, mcat, xcat, xw, bm = staged
        gs = slice(g * GROUP_W, (g + 1) * GROUP_W)
        y = y + _dot(mcat, xcat)
        upd = _dot_tn(xw, bm)
        for e in range(HEADS_PER_GROUP):
            hd = 4 * g + e
            h_ref[0, hd] = h_ref[0, hd] * ealast[:, hd:hd + 1] + upd[e * HEADDIM:(e + 1) * HEADDIM, :]
        yn_scr[rows, gs] = _gated_group_norm(y, z_scr[rows, gs], ng_ref[:, gs]).astype(BF16)

    staged = stage_a(0)
    for g in range(N_GROUPS):
        following = stage_a(g + 1) if g + 1 < N_GROUPS else None
        stage_b(g, staged)
        staged = following


SSDP_T = 256
SSDP_RB = 64
SSD_PAD = 8


def _ssd_prompt_kernel(x_ref, gmix_ref, wz_ref, wx_ref, wbc_ref, wdt_ref, bdt_ref, wgb_ref, wssm_ref,
                       cw_ref, cb_ref, alog_ref, dexp_ref, ng_ref, ex_ref,
                       gb_ref, ncv_ref, h_ref,
                       xe, u_scr, xs_scr, bc_scr, z_scr, yn_scr, acumt_scr):
    t = SSDP_T

    @pl.when(pl.program_id(1) == 0)
    def _():
        xe[:, 0:SSD_PAD, :] = jnp.zeros((SSM_SLABS, SSD_PAD, LANES), F32)
        h_ref[...] = jnp.zeros(h_ref.shape, F32)

    u_scr[...] = _rmsnorm(x_ref[0], gmix_ref[...]).astype(BF16)
    a_neg = -jnp.exp(alog_ref[...])
    off = SSD_PAD - SSM_HALO
    for c in range(t // CHUNK):
        c0 = c * CHUNK
        rows = slice(c0, c0 + CHUNK)
        u_c = u_scr[rows, :]
        z_scr[rows, :] = _dot(u_c, wz_ref[...])
        dt = _softplus(_dot(u_c, wdt_ref[...]) + bdt_ref[...])
        xr = _dot(u_c, wx_ref[...])
        for j in range(X_SLABS):
            xe[j, SSD_PAD + c0:SSD_PAD + c0 + CHUNK, :] = xr[:, _lanes(j)]
        bcr = _dot(u_c, wbc_ref[...])
        for j in range(X_SLABS):
            xe[X_SLABS + j, SSD_PAD + c0:SSD_PAD + c0 + CHUNK, :] = bcr[:, _lanes(j)]
        for j in range(SSM_SLABS):
            for rb in range(CHUNK // SSDP_RB):
                r0 = c0 + rb * SSDP_RB
                acc = jnp.broadcast_to(cb_ref[:, _lanes(j)], (SSDP_RB, LANES))
                for k in range(SSM_CONV):
                    acc = acc + cw_ref[k:k + 1, _lanes(j)] * xe[j, r0 + off + k:r0 + off + k + SSDP_RB, :]
                if j < X_SLABS:
                    xs_scr[r0:r0 + SSDP_RB, _lanes(j)] = _silu(acc)
                else:
                    bc_scr[r0:r0 + SSDP_RB, _lanes(j - X_SLABS)] = _silu(acc).astype(BF16)
        _ssd_chunk(rows, dt, xs_scr, bc_scr, z_scr, yn_scr, h_ref, a_neg, dexp_ref, ng_ref, ex_ref, acumt_scr)
        gb_ref[0, rows, :] = _sigmoid(_dot(u_c, wgb_ref[...])) * _dot(yn_scr[rows, :], wssm_ref[...])
    for j in range(SSM_SLABS):
        ncv_ref[0, :, _lanes(j)] = xe[j, t + off:t + SSD_PAD, :]
        xe[j, 0:SSD_PAD, :] = xe[j, t:t + SSD_PAD, :]


def _ssd_prompt(x3, g_mix, w_main, w_dt, b_dt, w_ssm, conv_w, conv_b, a_log, d_exp, norm_g, ex):
    nb, seq, _ = x3.shape
    t = SSDP_T
    wide = (D_MODEL, D_INNER)
    return pl.pallas_call(
        _ssd_prompt_kernel,
        grid=(nb, seq // t),
        in_specs=[
            pl.BlockSpec((1, t, D_MODEL), lambda b, i: (b, i, 0)),
            _const_spec((1, D_MODEL)),
            _const_spec(wide, (0, 2)), _const_spec(wide, (0, 3)), _const_spec(wide, (0, 4)),
            _const_spec((D_MODEL, DT_PAD)), _const_spec((1, DT_PAD)),
            _const_spec((D_MODEL, D_MODEL), (0, 3)),
            _const_spec((D_INNER, D_MODEL)),
            _const_spec((SSM_CONV, CONV_DIM)), _const_spec((1, CONV_DIM)), _const_spec((1, DT_PAD)),
            _const_spec((1, D_INNER)), _const_spec((1, D_INNER)),
            _const_spec((3 * DT_PAD, D_INNER)),
        ],
        out_specs=[
            pl.BlockSpec((1, t, D_MODEL), lambda b, i: (b, i, 0)),
            pl.BlockSpec((1, SSM_HALO, CONV_DIM), lambda b, i: (b, 0, 0)),
            pl.BlockSpec((1, N_HEADS, HEADDIM, D_STATE), lambda b, i: (b, 0, 0, 0)),
        ],
        out_shape=[
            jax.ShapeDtypeStruct((nb, seq, D_MODEL), F32),
            jax.ShapeDtypeStruct((nb, SSM_HALO, CONV_DIM), F32),
            jax.ShapeDtypeStruct((nb, N_HEADS, HEADDIM, D_STATE), F32),
        ],
        scratch_shapes=[
            pltpu.VMEM((SSM_SLABS, t + SSD_PAD, LANES), F32),
            pltpu.VMEM((t, D_MODEL), BF16),
            pltpu.VMEM((t, D_INNER), F32),
            pltpu.VMEM((t, D_INNER), BF16),
            pltpu.VMEM((t, D_INNER), F32),
            pltpu.VMEM((t, D_INNER), BF16),
            pltpu.VMEM((DT_PAD, CHUNK), F32),
        ],
        compiler_params=_params(2),
        name="ssd_prompt",
    )(x3, g_mix, w_main, w_main, w_main, w_dt, b_dt, w_main, w_ssm, conv_w, conv_b, a_log, d_exp, norm_g,
      ex)


def _tail_prompt_kernel(x_ref, ga_ref, gb_ref, wout_ref, gffn_ref, wup_ref, wdown_ref, gfin_ref, y_ref):
    merged = (ga_ref[...] + gb_ref[...]).astype(BF16)
    h = x_ref[...] + _dot(merged, wout_ref[...])
    y_ref[...] = _mlp_final(h, gffn_ref, wup_ref, wdown_ref, gfin_ref)


def _tail_prompt(x2d, ga, gb, w_out, g_ffn, w_up, w_down, g_final, tm=512):
    m = x2d.shape[0]
    tile = pl.BlockSpec((tm, D_MODEL), lambda i: (i, 0))
    vec = _const_spec((1, D_MODEL))
    return pl.pallas_call(
        _tail_prompt_kernel,
        grid=(m // tm,),
        in_specs=[tile, tile, tile, _const_spec((D_MODEL, D_MODEL)), vec,
                  _const_spec((D_MODEL, D_FF)), _const_spec((D_FF, D_MODEL)), vec],
        out_specs=tile,
        out_shape=jax.ShapeDtypeStruct((m, D_MODEL), F32),
        compiler_params=_params(1),
        name="tail_prompt",
    )(x2d, ga, gb, w_out, g_ffn, w_up, w_down, g_final)


def _inproj_kernel(x_ref, g_ref, w_ref, wdt_ref, bdt_ref, proj_ref, dt_ref, u_scr):
    @pl.when(pl.program_id(0) == 0)
    def _():
        ub = _rmsnorm(x_ref[...], g_ref[...]).astype(BF16)
        u_scr[...] = ub
        dt_ref[...] = _softplus(_dot(ub, wdt_ref[...]) + bdt_ref[...])

    proj_ref[...] = _dot(u_scr[...], w_ref[...])


def _inproj(x2d, g_mix, w_main, w_dt, b_dt, tn=1024):
    m = x2d.shape[0]
    return pl.pallas_call(
        _inproj_kernel,
        grid=(MAIN_COLS // tn,),
        in_specs=[
            pl.BlockSpec((m, D_MODEL), lambda j: (0, 0)),
            pl.BlockSpec((1, D_MODEL), lambda j: (0, 0)),
            pl.BlockSpec((D_MODEL, tn), lambda j: (0, j)),
            pl.BlockSpec((D_MODEL, DT_PAD), lambda j: (0, 0)),
            pl.BlockSpec((1, DT_PAD), lambda j: (0, 0)),
        ],
        out_specs=[
            pl.BlockSpec((m, tn), lambda j: (0, j)),
            pl.BlockSpec((m, DT_PAD), lambda j: (0, 0)),
        ],
        out_shape=[
            jax.ShapeDtypeStruct((m, MAIN_COLS), F32),
            jax.ShapeDtypeStruct((m, DT_PAD), F32),
        ],
        scratch_shapes=[pltpu.VMEM((m, D_MODEL), BF16)],
        compiler_params=_params(1),
        name="inproj_sample",
    )(x2d, g_mix, w_main, w_dt, b_dt)


CONVA_SR = 32


def _conva_sample_kernel(a_ref, b_ref, st_ref, w_ref, bias_ref, lng_ref, lnb_ref,
                         ca_ref, nst_ref, g3, c3):
    nreq = st_ref.shape[2]
    steps = a_ref.shape[0] // nreq
    glu = a_ref[...] * _sigmoid(b_ref[...])
    for j in range(CONF_SLABS):
        g3[j] = glu[:, _lanes(j)]

    def plane(h, j):
        if h < CONF_HALO:
            return st_ref[0, h, :, _lanes(j)]
        return g3[j, pl.ds(h - CONF_HALO, nreq, stride=steps), :]

    for j in range(CONF_SLABS):
        for t in range(steps):
            acc = jnp.broadcast_to(bias_ref[:, _lanes(j)], (nreq, LANES))
            for k in range(CONF_KERNEL):
                acc = acc + w_ref[k:k + 1, _lanes(j)] * plane(t + k, j)
            c3[j, pl.ds(t, nreq, stride=steps), :] = acc
        for h in range(CONF_HALO):
            nst_ref[0, h, :, _lanes(j)] = plane(h + steps, j)
    conv = jnp.concatenate([c3[j] for j in range(CONF_SLABS)], axis=1)
    ca_ref[...] = _ln_swish(conv, lng_ref[...], lnb_ref[...]).astype(BF16)


def _conva_sample(proj2d, state_planes, steps, dw_w, dw_b, ln_g, ln_b):
    depth, _, nreq, _ = state_planes.shape
    r = CONVA_SR
    vec = pl.BlockSpec((1, C_CONV), lambda i: (0, 0))
    st_spec = pl.BlockSpec((depth, CONF_HALO, r, C_CONV), lambda i: (0, 0, i, 0))
    slab = pltpu.VMEM((CONF_SLABS, r * steps, LANES), F32)
    return pl.pallas_call(
        _conva_sample_kernel,
        grid=(nreq // r,),
        in_specs=[
            pl.BlockSpec((r * steps, C_CONV), lambda i: (i, 0)),
            pl.BlockSpec((r * steps, C_CONV), lambda i: (i, 1)),
            st_spec,
            pl.BlockSpec((CONF_KERNEL, C_CONV), lambda i: (0, 0)),
            vec, vec, vec,
        ],
        out_specs=[pl.BlockSpec((r * steps, C_CONV), lambda i: (i, 0)), st_spec],
        out_shape=[
            jax.ShapeDtypeStruct((nreq * steps, C_CONV), BF16),
            jax.ShapeDtypeStruct(state_planes.shape, F32),
        ],
        scratch_shapes=[slab, slab],
        compiler_params=_params(1),
        name="conva_sample",
    )(proj2d, proj2d, state_planes, dw_w, dw_b, ln_g, ln_b)


SSD_SR = 8
SSD_ST = 8
SSD_SROWS = 16


def _ssd_sample_kernel(z_ref, xr_ref, bcr_ref, dt_ref, cst_ref, h0_ref, cw_ref, cb_ref, alog_ref,
                       dexp_ref, ng_ref, yn_ref, ncv_ref, h_ref, xe, dt8, z8):
    nreq = cst_ref.shape[2]
    steps = z_ref.shape[0] // nreq
    t = SSD_ST
    xe[...] = jnp.zeros(xe.shape, F32)
    dt8[...] = jnp.zeros(dt8.shape, F32)
    z8[...] = jnp.zeros(z8.shape, F32)
    rows = lax.broadcasted_iota(jnp.int32, (t, 1), 0)
    live = rows < steps
    lo_half = lax.broadcasted_iota(jnp.int32, (t, LANES), 1) < HEADDIM
    a_neg = -jnp.exp(alog_ref[...])
    groups = range(N_GROUPS)

    def conv(j):
        acc = jnp.broadcast_to(cb_ref[:, _lanes(j)], (t, LANES))
        for k in range(SSM_CONV):
            acc = acc + cw_ref[k:k + 1, _lanes(j)] * xe[j, k:k + t, :]
        return jnp.where(live, _silu(acc), 0.0)

    for r in range(nreq):
        rr = slice(r * steps, (r + 1) * steps)
        xr = xr_ref[rr, :]
        bcr = bcr_ref[rr, :]
        for j in range(SSM_SLABS):
            for h in range(SSM_HALO):
                xe[j, h:h + 1, :] = cst_ref[0, h, r:r + 1, _lanes(j)]
            new = xr[:, _lanes(j)] if j < X_SLABS else bcr[:, _lanes(j - X_SLABS)]
            xe[j, SSM_HALO:SSM_HALO + steps, :] = new
            for h in range(SSM_HALO):
                ncv_ref[0, h, r:r + 1, _lanes(j)] = xe[j, steps + h:steps + h + 1, :]
        dt8[0:steps, :] = dt_ref[rr, :]
        z8[0:steps, :] = z_ref[rr, :]

        dt = dt8[...]
        a = dt * a_neg
        acum = jnp.zeros((t, DT_PAD), F32)
        for s in range(steps):
            acum = acum + jnp.where(rows >= s, a[s:s + 1, :], 0.0)
        ealast = jnp.exp(acum[t - 1:t, :])

        tiles = [conv(j) for j in range(SSM_SLABS)]
        xs = [jnp.concatenate([tiles[2 * g], tiles[2 * g + 1]], axis=1) for g in groups]
        bm = [tiles[X_SLABS + g] for g in groups]
        cm = [tiles[X_SLABS + N_GROUPS + g] for g in groups]
        scores = [_dot_nt(cm[g], bm[g]) for g in groups]
        yoff = [_dot_nt(cm[g], h0_ref[0, r, 4 * g:4 * g + 4].reshape(GROUP_W, D_STATE)) for g in groups]
        acx = [_expand_heads(acum, g, lo_half) for g in groups]
        xdt = [xs[g] * _expand_heads(dt, g, lo_half) for g in groups]
        ys = []
        for g in groups:
            gs = slice(g * GROUP_W, (g + 1) * GROUP_W)
            y = yoff[g] * jnp.exp(acx[g]) + xs[g] * dexp_ref[:, gs]
            for s in range(steps):
                decay = jnp.exp(jnp.where(rows >= s, acx[g] - acx[g][s:s + 1, :], -jnp.inf))
                y = y + (scores[g][:, s:s + 1] * decay) * xdt[g][s:s + 1, :]
            ys.append(y)
        upd = [_dot_tn(xdt[g] * jnp.exp(acx[g][t - 1:t, :] - acx[g]), bm[g]) for g in groups]
        for g in groups:
            for e in range(HEADS_PER_GROUP):
                hd = 4 * g + e
                h_ref[0, r, hd] = (h0_ref[0, r, hd] * ealast[:, hd:hd + 1]
                                   + upd[g][e * HEADDIM:(e + 1) * HEADDIM, :])
        for g in groups:
            gs = slice(g * GROUP_W, (g + 1) * GROUP_W)
            yn = _gated_group_norm(ys[g], z8[:, gs], ng_ref[:, gs])
            yn_ref[rr, gs] = yn[0:steps, :]


def _ssd_sample(proj2d, dt2d, conv_state, h0, steps, conv_w, conv_b, a_log, d_exp, norm_g):
    depth, nreq = conv_state.shape[0], conv_state.shape[2]
    r = SSD_SR
    cst_spec = pl.BlockSpec((depth, SSM_HALO, r, CONV_DIM), lambda i: (0, 0, i, 0))
    h_spec = pl.BlockSpec((depth, r, N_HEADS, HEADDIM, D_STATE), lambda i: (0, i, 0, 0, 0))
    return pl.pallas_call(
        _ssd_sample_kernel,
        grid=(nreq // r,),
        in_specs=[
            pl.BlockSpec((r * steps, D_INNER), lambda i: (i, 2)),
            pl.BlockSpec((r * steps, D_INNER), lambda i: (i, 3)),
            pl.BlockSpec((r * steps, D_INNER), lambda i: (i, 4)),
            pl.BlockSpec((r * steps, DT_PAD), lambda i: (i, 0)),
            cst_spec,
            h_spec,
            pl.BlockSpec((SSM_CONV, CONV_DIM), lambda i: (0, 0)),
            pl.BlockSpec((1, CONV_DIM), lambda i: (0, 0)),
            pl.BlockSpec((1, DT_PAD), lambda i: (0, 0)),
            pl.BlockSpec((1, D_INNER), lambda i: (0, 0)),
            pl.BlockSpec((1, D_INNER), lambda i: (0, 0)),
        ],
        out_specs=[pl.BlockSpec((r * steps, D_INNER), lambda i: (i, 0)), cst_spec, h_spec],
        out_shape=[
            jax.ShapeDtypeStruct((nreq * steps, D_INNER), F32),
            jax.ShapeDtypeStruct(conv_state.shape, F32),
            jax.ShapeDtypeStruct(h0.shape, F32),
        ],
        scratch_shapes=[
            pltpu.VMEM((SSM_SLABS, SSD_SROWS, LANES), F32),
            pltpu.VMEM((SSD_ST, DT_PAD), F32),
            pltpu.VMEM((SSD_ST, D_INNER), F32),
        ],
        compiler_params=_params(1),
        name="ssd_sample",
    )(proj2d, proj2d, proj2d, dt2d, conv_state, h0, conv_w, conv_b, a_log, d_exp, norm_g)


def _tail_sample_kernel(x_ref, ca_ref, yn_ref, gate_ref, wpw_ref, bpw_ref, wssm_ref, wout_ref, gffn_ref,
                        wup_ref, wdown_ref, gfin_ref, y_ref):
    branch_a = _dot(ca_ref[...], wpw_ref[...]) + bpw_ref[...]
    branch_b = _dot(yn_ref[...].astype(BF16), wssm_ref[...])
    merged = (_sigmoid(gate_ref[:, 0:D_MODEL]) * branch_a
              + _sigmoid(gate_ref[:, D_MODEL:2 * D_MODEL]) * branch_b)
    h = x_ref[...] + _dot(merged.astype(BF16), wout_ref[...])
    y_ref[...] = _mlp_final(h, gffn_ref, wup_ref, wdown_ref, gfin_ref)


def _tail_sample(x2d, ca, yn, proj2d, w_pw, b_pw, w_ssm, w_out, g_ffn, w_up, w_down, g_final):
    m = x2d.shape[0]
    vec = _const_spec((1, D_MODEL))
    return pl.pallas_call(
        _tail_sample_kernel,
        grid=(1,),
        in_specs=[
            pl.BlockSpec((m, D_MODEL), lambda i: (0, 0)),
            pl.BlockSpec((m, C_CONV), lambda i: (0, 0)),
            pl.BlockSpec((m, D_INNER), lambda i: (0, 0)),
            pl.BlockSpec((m, 2 * D_MODEL), lambda i: (0, 1)),
            _const_spec((C_CONV, D_MODEL)), vec, _const_spec((D_INNER, D_MODEL)),
            _const_spec((D_MODEL, D_MODEL)), vec, _const_spec((D_MODEL, D_FF)),
            _const_spec((D_FF, D_MODEL)), vec,
        ],
        out_specs=pl.BlockSpec((m, D_MODEL), lambda i: (0, 0)),
        out_shape=jax.ShapeDtypeStruct((m, D_MODEL), F32),
        compiler_params=_params(1),
        name="tail_sample",
    )(x2d, ca, yn, proj2d, w_pw, b_pw, w_ssm, w_out, g_ffn, w_up, w_down, g_final)


def kernel(x_prompt, x_sample, state_conf_conv, state_ssm_conv, state_ssm, g_mix, w_in, conf_dw_w,
           conf_dw_b, conf_ln_g, conf_ln_b, conf_w_pw, conf_b_pw, ssm_conv_w, ssm_conv_b, ssm_dt_bias,
           ssm_a_log, ssm_d, ssm_norm_g, ssm_w_out, w_out, g_ffn, w_up, w_down, g_final):
    depth = w_in.shape[0]
    assert depth == 1
    nb, seq, _ = x_prompt.shape
    nreq, steps, _ = x_sample.shape
    i = 0

    def row(v):
        return v.reshape(1, -1)

    def pad_lanes(v, width):
        return jnp.pad(v, ((0, 0), (0, width - v.shape[1])))

    gm = row(g_mix[i])
    w_main = w_in[i].astype(BF16)
    w_dt = pad_lanes(w_main[:, MAIN_COLS:], DT_PAD)
    b_dt = pad_lanes(row(ssm_dt_bias[i]), DT_PAD)
    a_log = pad_lanes(row(ssm_a_log[i]), DT_PAD)
    d_exp = row(jnp.repeat(ssm_d[i], HEADDIM))
    norm_g = row(ssm_norm_g[i])
    dw_w = conf_dw_w[i]
    dw_b, ln_g, ln_b = row(conf_dw_b[i]), row(conf_ln_g[i]), row(conf_ln_b[i])
    cw, cb = ssm_conv_w[i], row(ssm_conv_b[i])
    w_pw, b_pw = conf_w_pw[i].astype(BF16), row(conf_b_pw[i])
    w_ssm = ssm_w_out[i].astype(BF16)
    mlp_w = (w_out[i].astype(BF16), row(g_ffn[i]), w_up[i].astype(BF16), w_down[i].astype(BF16),
             row(g_final))

    ga, conf_p = _conva_prompt(x_prompt, gm, w_main, dw_w, dw_b, ln_g, ln_b, w_pw, b_pw)
    head_row = jnp.arange(3 * DT_PAD, dtype=jnp.int32)[:, None] % DT_PAD
    ex = (head_row == jnp.arange(D_INNER, dtype=jnp.int32)[None, :] // HEADDIM).astype(BF16)
    gb, scv_p, h_p = _ssd_prompt(x_prompt, gm, w_main, w_dt, b_dt, w_ssm, cw, cb, a_log, d_exp, norm_g, ex)
    m_p = nb * seq
    y_p = _tail_prompt(x_prompt.reshape(m_p, D_MODEL), ga.reshape(m_p, D_MODEL), gb.reshape(m_p, D_MODEL),
                       *mlp_w)

    xs = x_sample.reshape(nreq * steps, D_MODEL)
    proj_s, dt_s = _inproj(xs, gm, w_main, w_dt, b_dt)
    planes = (0, 2, 1, 3)
    ca_s, conf_s = _conva_sample(proj_s, state_conf_conv.transpose(planes), steps, dw_w, dw_b, ln_g, ln_b)
    yn_s, scv_s, h_s = _ssd_sample(proj_s, dt_s, state_ssm_conv.transpose(planes), state_ssm, steps, cw, cb,
                                   a_log, d_exp, norm_g)
    y_s = _tail_sample(xs, ca_s, yn_s, proj_s, w_pw, b_pw, w_ssm, *mlp_w)

    return (y_p.reshape(nb, seq, D_MODEL), y_s.reshape(nreq, steps, D_MODEL),
            conf_p[None], scv_p[None], h_p[None], conf_s.transpose(planes), scv_s.transpose(planes), h_s)
```

```python
import jax
import jax.numpy as jnp
from jax import lax
from jax.experimental import pallas as pl
from jax.experimental.pallas import tpu as pltpu

F32 = jnp.float32
BF16 = jnp.bfloat16

D_MODEL = 1024
C_CONV = 1024
CONF_KERNEL = 31
CONF_HALO = CONF_KERNEL - 1
D_INNER = 2048
HEADDIM = 64
N_HEADS = 32
N_GROUPS = 8
HEADS_PER_GROUP = 4
GROUP_W = HEADS_PER_GROUP * HEADDIM
D_STATE = 128
SSM_CONV = 4
SSM_HALO = SSM_CONV - 1
CONV_DIM = 4096
CHUNK = 128
D_FF = 4096
EPS = 1e-6
MAIN_COLS = 10240
DT_PAD = 128
LANES = 128
CONF_SLABS = C_CONV // LANES
SSM_SLABS = CONV_DIM // LANES
X_SLABS = D_INNER // LANES
VMEM_LIMIT = 56 * 1024 * 1024


def _sigmoid(x):
    return jax.nn.sigmoid(x)


def _silu(x):
    return x * jax.nn.sigmoid(x)


def _softplus(x):
    return jnp.maximum(x, 0.0) + jnp.log1p(jnp.exp(-jnp.abs(x)))


def _rmsnorm(x, g):
    return x * lax.rsqrt(jnp.mean(x * x, axis=-1, keepdims=True) + EPS) * g


def _dot(a, b):
    return jnp.dot(a, b, preferred_element_type=F32)


def _dot_nt(a, b):
    return lax.dot_general(a, b, (((1,), (1,)), ((), ())), preferred_element_type=F32)


def _dot_tn(a, b):
    return lax.dot_general(a, b, (((0,), (0,)), ((), ())), preferred_element_type=F32)


def _split3(x):
    hi = x.astype(BF16)
    r1 = x - hi.astype(F32)
    mid = r1.astype(BF16)
    lo = (r1 - mid.astype(F32)).astype(BF16)
    return hi, mid, lo


def _cumsum_rows(a, tri):
    hi, mid, lo = _split3(a)
    return _dot(tri, hi) + _dot(tri, mid) + _dot(tri, lo)


def _lanes(j):
    return slice(j * LANES, (j + 1) * LANES)


def _const_spec(shape, index=None):
    nd = len(shape)
    idx = index if index is not None else (0,) * nd
    return pl.BlockSpec(shape, lambda *_: idx, pipeline_mode=pl.Buffered(1))


def _params(ngrid):
    return pltpu.CompilerParams(dimension_semantics=("arbitrary",) * ngrid, vmem_limit_bytes=VMEM_LIMIT)


def _ln_swish(y, g, b):
    mu = jnp.mean(y, axis=-1, keepdims=True)
    yc = y - mu
    yn = yc * lax.rsqrt(jnp.mean(yc * yc, axis=-1, keepdims=True) + EPS) * g + b
    return _silu(yn)


def _mlp_final(h, gffn_ref, wup_ref, wdown_ref, gfin_ref):
    hidden = jnp.square(jnp.maximum(_dot(_rmsnorm(h, gffn_ref[...]).astype(BF16), wup_ref[...]), 0.0))
    h = h + _dot(hidden.astype(BF16), wdown_ref[...])
    return _rmsnorm(h, gfin_ref[...])


CONVA_T = 512
CONVA_RB = 64
CONVA_PAD = 32


def _conva_prompt_kernel(x_ref, gmix_ref, wa_ref, wb_ref, wga_ref, dw_ref, dwb_ref, lng_ref, lnb_ref,
                         wpw_ref, bpw_ref, ga_ref, st_ref, xe, conv, u_scr):
    t = CONVA_T

    @pl.when(pl.program_id(1) == 0)
    def _():
        xe[:, 0:CONVA_PAD, :] = jnp.zeros((CONF_SLABS, CONVA_PAD, LANES), F32)

    u_scr[...] = _rmsnorm(x_ref[0], gmix_ref[...]).astype(BF16)
    off = CONVA_PAD - CONF_HALO
    glu = _dot(u_scr[...], wa_ref[...]) * _sigmoid(_dot(u_scr[...], wb_ref[...]))
    for j in range(CONF_SLABS):
        xe[j, CONVA_PAD:CONVA_PAD + t, :] = glu[:, _lanes(j)]
    for j in range(CONF_SLABS):
        for rb in range(t // CONVA_RB):
            r0 = rb * CONVA_RB
            acc = jnp.broadcast_to(dwb_ref[:, _lanes(j)], (CONVA_RB, LANES))
            for k in range(CONF_KERNEL):
                acc = acc + dw_ref[k:k + 1, _lanes(j)] * xe[j, r0 + off + k:r0 + off + k + CONVA_RB, :]
            conv[r0:r0 + CONVA_RB, _lanes(j)] = acc
    ca = _ln_swish(conv[...], lng_ref[...], lnb_ref[...]).astype(BF16)
    ga_ref[0] = _sigmoid(_dot(u_scr[...], wga_ref[...])) * (_dot(ca, wpw_ref[...]) + bpw_ref[...])
    for j in range(CONF_SLABS):
        st_ref[0, :, _lanes(j)] = xe[j, t + off:t + CONVA_PAD, :]
        xe[j, 0:CONVA_PAD, :] = xe[j, t:t + CONVA_PAD, :]


def _conva_prompt(x3, g_mix, w_main, dw_w, dw_b, ln_g, ln_b, w_pw, b_pw):
    nb, seq, _ = x3.shape
    t = CONVA_T
    vec = _const_spec((1, C_CONV))
    sq = (D_MODEL, D_MODEL)
    return pl.pallas_call(
        _conva_prompt_kernel,
        grid=(nb, seq // t),
        in_specs=[
            pl.BlockSpec((1, t, D_MODEL), lambda b, i: (b, i, 0)),
            vec,
            _const_spec(sq, (0, 0)), _const_spec(sq, (0, 1)), _const_spec(sq, (0, 2)),
            _const_spec((CONF_KERNEL, C_CONV)), vec, vec, vec,
            _const_spec(sq), vec,
        ],
        out_specs=[
            pl.BlockSpec((1, t, D_MODEL), lambda b, i: (b, i, 0)),
            pl.BlockSpec((1, CONF_HALO, C_CONV), lambda b, i: (b, 0, 0)),
        ],
        out_shape=[
            jax.ShapeDtypeStruct((nb, seq, D_MODEL), F32),
            jax.ShapeDtypeStruct((nb, CONF_HALO, C_CONV), F32),
        ],
        scratch_shapes=[
            pltpu.VMEM((CONF_SLABS, t + CONVA_PAD, LANES), F32),
            pltpu.VMEM((t, C_CONV), F32),
            pltpu.VMEM((t, D_MODEL), BF16),
        ],
        compiler_params=_params(2),
        name="conva_prompt",
    )(x3, g_mix, w_main, w_main, w_main, dw_w, dw_b, ln_g, ln_b, w_pw, b_pw)


def _expand_heads(v, g, lo_half):
    rows = v.shape[0]

    def col(e):
        return jnp.broadcast_to(v[:, 4 * g + e:4 * g + e + 1], (rows, LANES))

    return jnp.concatenate([jnp.where(lo_half, col(0), col(1)), jnp.where(lo_half, col(2), col(3))], axis=1)


def _gated_group_norm(y, z, g):
    y = y * _silu(z)
    return y * lax.rsqrt(jnp.mean(y * y, axis=-1, keepdims=True) + EPS) * g


LOG2E = 1.4426950408889634


def _cat3(v):
    return jnp.concatenate(_split3(v), axis=1)


def _ssd_chunk(rows, dt, xs_scr, bc_scr, z_scr, yn_scr, h_ref, a_neg, dexp_ref, ng_ref, ex_ref, acumt_scr):
    t = CHUNK
    r_i = lax.broadcasted_iota(jnp.int32, (t, t), 0)
    c_i = lax.broadcasted_iota(jnp.int32, (t, t), 1)
    causal = r_i >= c_i
    tri = causal.astype(BF16)
    lo_half = lax.broadcasted_iota(jnp.int32, (t, LANES), 1) < HEADDIM
    zeros = jnp.zeros((t, LANES), F32)
    zeros_b = jnp.zeros((t, t), BF16)

    acum = _cumsum_rows(dt * a_neg, tri)
    a_last = acum[t - 1:t, :]
    ealast = jnp.exp(a_last)
    acum2 = acum * LOG2E
    acumt_scr[...] = acum2.T
    dt3 = _cat3(dt)

    def stage_a(g):
        gs = slice(g * GROUP_W, (g + 1) * GROUP_W)
        xs = xs_scr[rows, gs]
        bm = bc_scr[rows, _lanes(g)]
        cm = bc_scr[rows, _lanes(N_GROUPS + g)]
        scores = _dot_nt(cm, bm)
        h_g = h_ref[0, 4 * g:4 * g + 4].reshape(GROUP_W, D_STATE)
        colb = [jnp.broadcast_to(acum2[:, 4 * g + e:4 * g + e + 1], (t, LANES))
                for e in range(HEADS_PER_GROUP)]
        acx = jnp.concatenate([jnp.where(lo_half, colb[0], colb[1]),
                               jnp.where(lo_half, colb[2], colb[3])], axis=1)
        xdt = xs * _dot(dt3, ex_ref[:, gs])
        y = _dot_nt(cm, h_g.astype(BF16)) * jnp.exp2(acx) + xs * dexp_ref[:, gs]
        ms, xblk = [], []
        for e in range(HEADS_PER_GROUP):
            hd = 4 * g + e
            seg = colb[e] - acumt_scr[hd:hd + 1, :]
            m = (scores * jnp.exp2(seg)).astype(BF16)
            ms.append(jnp.where(causal, m, zeros_b))
            half = xdt[:, _lanes(e // 2)]
            keep = jnp.where(lo_half, half, 0.0) if e % 2 == 0 else jnp.where(lo_half, 0.0, half)
            blk = jnp.concatenate([keep, zeros] if e < 2 else [zeros, keep], axis=1)
            xblk.append(blk.astype(BF16))
        xw = (xdt * jnp.exp2(acx[t - 1:t, :] - acx)).astype(BF16)
        return y, jnp.concatenate(ms, axis=1), jnp.concatenate(xblk, axis=0), xw, bm

    def stage_b(g, staged):
        y, mcat, xcat, xw, bm = staged
        gs = slice(g * GROUP_W, (g + 1) * GROUP_W)
        y = y + _dot(mcat, xcat)
        upd = _dot_tn(xw, bm)
        for e in range(HEADS_PER_GROUP):
            hd = 4 * g + e
            h_ref[0, hd] = h_ref[0, hd] * ealast[:, hd:hd + 1] + upd[e * HEADDIM:(e + 1) * HEADDIM, :]
        yn_scr[rows, gs] = _gated_group_norm(y, z_scr[rows, gs], ng_ref[:, gs]).astype(BF16)

    staged = stage_a(0)
    for g in range(N_GROUPS):
        following = stage_a(g + 1) if g + 1 < N_GROUPS else None
        stage_b(g, staged)
        staged = following


SSDP_T = 256
SSDP_RB = 64
SSD_PAD = 8


def _ssd_prompt_kernel(x_ref, gmix_ref, wz_ref, wx_ref, wbc_ref, wdt_ref, bdt_ref, wgb_ref, wssm_ref,
                       cw_ref, cb_ref, alog_ref, dexp_ref, ng_ref, ex_ref,
                       gb_ref, ncv_ref, h_ref,
                       xe, u_scr, xs_scr, bc_scr, z_scr, yn_scr, acumt_scr):
    t = SSDP_T

    @pl.when(pl.program_id(1) == 0)
    def _():
        xe[:, 0:SSD_PAD, :] = jnp.zeros((SSM_SLABS, SSD_PAD, LANES), F32)
        h_ref[...] = jnp.zeros(h_ref.shape, F32)

    u_scr[...] = _rmsnorm(x_ref[0], gmix_ref[...]).astype(BF16)
    a_neg = -jnp.exp(alog_ref[...])
    off = SSD_PAD - SSM_HALO
    for c in range(t // CHUNK):
        c0 = c * CHUNK
        rows = slice(c0, c0 + CHUNK)
        u_c = u_scr[rows, :]
        z_scr[rows, :] = _dot(u_c, wz_ref[...])
        dt = _softplus(_dot(u_c, wdt_ref[...].astype(BF16)) + bdt_ref[...])
        xr = _dot(u_c, wx_ref[...])
        for j in range(X_SLABS):
            xe[j, SSD_PAD + c0:SSD_PAD + c0 + CHUNK, :] = xr[:, _lanes(j)]
        bcr = _dot(u_c, wbc_ref[...])
        for j in range(X_SLABS):
            xe[X_SLABS + j, SSD_PAD + c0:SSD_PAD + c0 + CHUNK, :] = bcr[:, _lanes(j)]
        for j in range(SSM_SLABS):
            for rb in range(CHUNK // SSDP_RB):
                r0 = c0 + rb * SSDP_RB
                acc = jnp.broadcast_to(cb_ref[:, _lanes(j)], (SSDP_RB, LANES))
                for k in range(SSM_CONV):
                    acc = acc + cw_ref[k:k + 1, _lanes(j)] * xe[j, r0 + off + k:r0 + off + k + SSDP_RB, :]
                if j < X_SLABS:
                    xs_scr[r0:r0 + SSDP_RB, _lanes(j)] = _silu(acc)
                else:
                    bc_scr[r0:r0 + SSDP_RB, _lanes(j - X_SLABS)] = _silu(acc).astype(BF16)
        _ssd_chunk(rows, dt, xs_scr, bc_scr, z_scr, yn_scr, h_ref, a_neg, dexp_ref, ng_ref, ex_ref, acumt_scr)
        gb_ref[0, rows, :] = _sigmoid(_dot(u_c, wgb_ref[...])) * _dot(yn_scr[rows, :], wssm_ref[...])
    for j in range(SSM_SLABS):
        ncv_ref[0, :, _lanes(j)] = xe[j, t + off:t + SSD_PAD, :]
        xe[j, 0:SSD_PAD, :] = xe[j, t:t + SSD_PAD, :]


def _ssd_prompt(x3, g_mix, w_main, w_dt, b_dt, w_ssm, conv_w, conv_b, a_log, d_exp, norm_g, ex):
    nb, seq, _ = x3.shape
    t = SSDP_T
    wide = (D_MODEL, D_INNER)
    return pl.pallas_call(
        _ssd_prompt_kernel,
        grid=(nb, seq // t),
        in_specs=[
            pl.BlockSpec((1, t, D_MODEL), lambda b, i: (b, i, 0)),
            _const_spec((1, D_MODEL)),
            _const_spec(wide, (0, 2)), _const_spec(wide, (0, 3)), _const_spec(wide, (0, 4)),
            _const_spec((D_MODEL, DT_PAD)), _const_spec((1, DT_PAD)),
            _const_spec((D_MODEL, D_MODEL), (0, 3)),
            _const_spec((D_INNER, D_MODEL)),
            _const_spec((SSM_CONV, CONV_DIM)), _const_spec((1, CONV_DIM)), _const_spec((1, DT_PAD)),
            _const_spec((1, D_INNER)), _const_spec((1, D_INNER)),
            _const_spec((3 * DT_PAD, D_INNER)),
        ],
        out_specs=[
            pl.BlockSpec((1, t, D_MODEL), lambda b, i: (b, i, 0)),
            pl.BlockSpec((1, SSM_HALO, CONV_DIM), lambda b, i: (b, 0, 0)),
            pl.BlockSpec((1, N_HEADS, HEADDIM, D_STATE), lambda b, i: (b, 0, 0, 0)),
        ],
        out_shape=[
            jax.ShapeDtypeStruct((nb, seq, D_MODEL), F32),
            jax.ShapeDtypeStruct((nb, SSM_HALO, CONV_DIM), F32),
            jax.ShapeDtypeStruct((nb, N_HEADS, HEADDIM, D_STATE), F32),
        ],
        scratch_shapes=[
            pltpu.VMEM((SSM_SLABS, t + SSD_PAD, LANES), F32),
            pltpu.VMEM((t, D_MODEL), BF16),
            pltpu.VMEM((t, D_INNER), F32),
            pltpu.VMEM((t, D_INNER), BF16),
            pltpu.VMEM((t, D_INNER), F32),
            pltpu.VMEM((t, D_INNER), BF16),
            pltpu.VMEM((DT_PAD, CHUNK), F32),
        ],
        compiler_params=_params(2),
        name="ssd_prompt",
    )(x3, g_mix, w_main, w_main, w_main, w_dt, b_dt, w_main, w_ssm, conv_w, conv_b, a_log, d_exp, norm_g,
      ex)


def _tail_prompt_kernel(x_ref, ga_ref, gb_ref, wout_ref, gffn_ref, wup_ref, wdown_ref, gfin_ref, y_ref):
    merged = (ga_ref[...] + gb_ref[...]).astype(BF16)
    h = x_ref[...] + _dot(merged, wout_ref[...])
    y_ref[...] = _mlp_final(h, gffn_ref, wup_ref, wdown_ref, gfin_ref)


def _tail_prompt(x2d, ga, gb, w_out, g_ffn, w_up, w_down, g_final, tm=512):
    m = x2d.shape[0]
    tile = pl.BlockSpec((tm, D_MODEL), lambda i: (i, 0))
    vec = _const_spec((1, D_MODEL))
    return pl.pallas_call(
        _tail_prompt_kernel,
        grid=(m // tm,),
        in_specs=[tile, tile, tile, _const_spec((D_MODEL, D_MODEL)), vec,
                  _const_spec((D_MODEL, D_FF)), _const_spec((D_FF, D_MODEL)), vec],
        out_specs=tile,
        out_shape=jax.ShapeDtypeStruct((m, D_MODEL), F32),
        compiler_params=_params(1),
        name="tail_prompt",
    )(x2d, ga, gb, w_out, g_ffn, w_up, w_down, g_final)


def _inproj_kernel(x_ref, g_ref, w_ref, wdt_ref, bdt_ref, proj_ref, dt_ref, wb_ref, u_scr):
    @pl.when(pl.program_id(0) == 0)
    def _():
        ub = _rmsnorm(x_ref[...], g_ref[...]).astype(BF16)
        u_scr[...] = ub
        dt_ref[...] = _softplus(_dot(ub, wdt_ref[...].astype(BF16)) + bdt_ref[...])

    wb = w_ref[...].T.astype(BF16)
    wb_ref[...] = wb
    proj_ref[...] = _dot(u_scr[...], wb)


def _inproj(x2d, g_mix, w_in_t, w_dt, b_dt, tn=1024):
    m = x2d.shape[0]
    return pl.pallas_call(
        _inproj_kernel,
        grid=(MAIN_COLS // tn,),
        in_specs=[
            pl.BlockSpec((m, D_MODEL), lambda j: (0, 0)),
            pl.BlockSpec((1, D_MODEL), lambda j: (0, 0)),
            pl.BlockSpec((tn, D_MODEL), lambda j: (j, 0)),
            pl.BlockSpec((D_MODEL, DT_PAD), lambda j: (0, 0)),
            pl.BlockSpec((1, DT_PAD), lambda j: (0, 0)),
        ],
        out_specs=[
            pl.BlockSpec((m, tn), lambda j: (0, j)),
            pl.BlockSpec((m, DT_PAD), lambda j: (0, 0)),
            pl.BlockSpec((D_MODEL, tn), lambda j: (0, j)),
        ],
        out_shape=[
            jax.ShapeDtypeStruct((m, MAIN_COLS), F32),
            jax.ShapeDtypeStruct((m, DT_PAD), F32),
            jax.ShapeDtypeStruct((D_MODEL, MAIN_COLS), BF16),
        ],
        scratch_shapes=[pltpu.VMEM((m, D_MODEL), BF16)],
        compiler_params=_params(1),
        name="inproj_sample",
    )(x2d, g_mix, w_in_t, w_dt, b_dt)


CONVA_SR = 32


def _conva_sample_kernel(a_ref, b_ref, st_ref, w_ref, bias_ref, lng_ref, lnb_ref,
                         ca_ref, nst_ref, g3, c3):
    nreq = st_ref.shape[2]
    steps = a_ref.shape[0] // nreq
    glu = a_ref[...] * _sigmoid(b_ref[...])
    for j in range(CONF_SLABS):
        g3[j] = glu[:, _lanes(j)]

    def plane(h, j):
        if h < CONF_HALO:
            return st_ref[0, h, :, _lanes(j)]
        return g3[j, pl.ds(h - CONF_HALO, nreq, stride=steps), :]

    for j in range(CONF_SLABS):
        for t in range(steps):
            acc = jnp.broadcast_to(bias_ref[:, _lanes(j)], (nreq, LANES))
            for k in range(CONF_KERNEL):
                acc = acc + w_ref[k:k + 1, _lanes(j)] * plane(t + k, j)
            c3[j, pl.ds(t, nreq, stride=steps), :] = acc
        for h in range(CONF_HALO):
            nst_ref[0, h, :, _lanes(j)] = plane(h + steps, j)
    conv = jnp.concatenate([c3[j] for j in range(CONF_SLABS)], axis=1)
    ca_ref[...] = _ln_swish(conv, lng_ref[...], lnb_ref[...]).astype(BF16)


def _conva_sample(proj2d, state_planes, steps, dw_w, dw_b, ln_g, ln_b):
    depth, _, nreq, _ = state_planes.shape
    r = CONVA_SR
    vec = pl.BlockSpec((1, C_CONV), lambda i: (0, 0))
    st_spec = pl.BlockSpec((depth, CONF_HALO, r, C_CONV), lambda i: (0, 0, i, 0))
    slab = pltpu.VMEM((CONF_SLABS, r * steps, LANES), F32)
    return pl.pallas_call(
        _conva_sample_kernel,
        grid=(nreq // r,),
        in_specs=[
            pl.BlockSpec((r * steps, C_CONV), lambda i: (i, 0)),
            pl.BlockSpec((r * steps, C_CONV), lambda i: (i, 1)),
            st_spec,
            pl.BlockSpec((CONF_KERNEL, C_CONV), lambda i: (0, 0)),
            vec, vec, vec,
        ],
        out_specs=[pl.BlockSpec((r * steps, C_CONV), lambda i: (i, 0)), st_spec],
        out_shape=[
            jax.ShapeDtypeStruct((nreq * steps, C_CONV), BF16),
            jax.ShapeDtypeStruct(state_planes.shape, F32),
        ],
        scratch_shapes=[slab, slab],
        compiler_params=_params(1),
        name="conva_sample",
    )(proj2d, proj2d, state_planes, dw_w, dw_b, ln_g, ln_b)


SSD_SR = 8
SSD_ST = 8
SSD_SROWS = 16


def _ssd_sample_kernel(z_ref, xr_ref, bcr_ref, dt_ref, cst_ref, h0_ref, cw_ref, cb_ref, alog_ref,
                       dexp_ref, ng_ref, yn_ref, ncv_ref, h_ref, xe_all, dt8, z8):
    nreq = cst_ref.shape[2]
    steps = z_ref.shape[0] // nreq
    t = SSD_ST
    xe_all[...] = jnp.zeros(xe_all.shape, F32)
    dt8[...] = jnp.zeros(dt8.shape, F32)
    z8[...] = jnp.zeros(z8.shape, F32)
    rows = lax.broadcasted_iota(jnp.int32, (t, 1), 0)
    live = rows < steps
    lo_half = lax.broadcasted_iota(jnp.int32, (t, LANES), 1) < HEADDIM
    a_neg = -jnp.exp(alog_ref[...])
    groups = range(N_GROUPS)

    def request_phases(r, slot):
        rr = slice(r * steps, (r + 1) * steps)
        xs_ = xe_all.at[slot]
        v = {}

        def conv(j):
            acc = jnp.broadcast_to(cb_ref[:, _lanes(j)], (t, LANES))
            for k in range(SSM_CONV):
                acc = acc + cw_ref[k:k + 1, _lanes(j)] * xs_[j, k:k + t, :]
            return jnp.where(live, _silu(acc), 0.0)

        def p_load():
            xr = xr_ref[rr, :]
            bcr = bcr_ref[rr, :]
            for j in range(SSM_SLABS):
                for h in range(SSM_HALO):
                    xs_[j, h:h + 1, :] = cst_ref[0, h, r:r + 1, _lanes(j)]
                new = xr[:, _lanes(j)] if j < X_SLABS else bcr[:, _lanes(j - X_SLABS)]
                xs_[j, SSM_HALO:SSM_HALO + steps, :] = new
                for h in range(SSM_HALO):
                    ncv_ref[0, h, r:r + 1, _lanes(j)] = xs_[j, steps + h:steps + h + 1, :]
            dt8[slot, 0:steps, :] = dt_ref[rr, :]
            z8[slot, 0:steps, :] = z_ref[rr, :]

        def p_conv():
            dt = dt8[slot]
            a = dt * a_neg
            acum = jnp.zeros((t, DT_PAD), F32)
            for s in range(steps):
                acum = acum + jnp.where(rows >= s, a[s:s + 1, :], 0.0)
            tiles = [conv(j) for j in range(SSM_SLABS)]
            v.update(dt=dt, acum=acum, ealast=jnp.exp(acum[t - 1:t, :]),
                     xs=[jnp.concatenate([tiles[2 * g], tiles[2 * g + 1]], axis=1) for g in groups],
                     bm=[tiles[X_SLABS + g] for g in groups],
                     cm=[tiles[X_SLABS + N_GROUPS + g] for g in groups])

        def p_small_products():
            v["scores"] = [_dot_nt(v["cm"][g], v["bm"][g]) for g in groups]
            v["yoff"] = [_dot_nt(v["cm"][g], h0_ref[0, r, 4 * g:4 * g + 4].reshape(GROUP_W, D_STATE))
                         for g in groups]

        def p_spread():
            v["acx"] = [_expand_heads(v["acum"], g, lo_half) for g in groups]
            v["xdt"] = [v["xs"][g] * _expand_heads(v["dt"], g, lo_half) for g in groups]

        def p_vector():
            ys = []
            for g in groups:
                gs = slice(g * GROUP_W, (g + 1) * GROUP_W)
                acx, xdt, scores = v["acx"][g], v["xdt"][g], v["scores"][g]
                y = v["yoff"][g] * jnp.exp(acx) + v["xs"][g] * dexp_ref[:, gs]
                for s in range(steps):
                    decay = jnp.exp(jnp.where(rows >= s, acx - acx[s:s + 1, :], -jnp.inf))
                    y = y + (scores[:, s:s + 1] * decay) * xdt[s:s + 1, :]
                ys.append(y)
            v["ys"] = ys

        def p_state_products():
            v["upd"] = [_dot_tn(v["xdt"][g] * jnp.exp(v["acx"][g][t - 1:t, :] - v["acx"][g]), v["bm"][g])
                        for g in groups]

        def p_state_store():
            for g in groups:
                for e in range(HEADS_PER_GROUP):
                    hd = 4 * g + e
                    h_ref[0, r, hd] = (h0_ref[0, r, hd] * v["ealast"][:, hd:hd + 1]
                                       + v["upd"][g][e * HEADDIM:(e + 1) * HEADDIM, :])

        def p_out():
            for g in groups:
                gs = slice(g * GROUP_W, (g + 1) * GROUP_W)
                yn = _gated_group_norm(v["ys"][g], z8[slot, :, gs], ng_ref[:, gs])
                yn_ref[rr, gs] = yn[0:steps, :]

        return [p_load, p_conv, p_small_products, p_spread, p_vector, p_state_products, p_state_store, p_out]

    for pair in range(nreq // 2):
        for pa, pb in zip(request_phases(2 * pair, 0), request_phases(2 * pair + 1, 1)):
            pa()
            pb()


def _ssd_sample(proj2d, dt2d, conv_state, h0, steps, conv_w, conv_b, a_log, d_exp, norm_g):
    depth, nreq = conv_state.shape[0], conv_state.shape[2]
    r = SSD_SR
    cst_spec = pl.BlockSpec((depth, SSM_HALO, r, CONV_DIM), lambda i: (0, 0, i, 0))
    h_spec = pl.BlockSpec((depth, r, N_HEADS, HEADDIM, D_STATE), lambda i: (0, i, 0, 0, 0))
    return pl.pallas_call(
        _ssd_sample_kernel,
        grid=(nreq // r,),
        in_specs=[
            pl.BlockSpec((r * steps, D_INNER), lambda i: (i, 2)),
            pl.BlockSpec((r * steps, D_INNER), lambda i: (i, 3)),
            pl.BlockSpec((r * steps, D_INNER), lambda i: (i, 4)),
            pl.BlockSpec((r * steps, DT_PAD), lambda i: (i, 0)),
            cst_spec,
            h_spec,
            pl.BlockSpec((SSM_CONV, CONV_DIM), lambda i: (0, 0)),
            pl.BlockSpec((1, CONV_DIM), lambda i: (0, 0)),
            pl.BlockSpec((1, DT_PAD), lambda i: (0, 0)),
            pl.BlockSpec((1, D_INNER), lambda i: (0, 0)),
            pl.BlockSpec((1, D_INNER), lambda i: (0, 0)),
        ],
        out_specs=[pl.BlockSpec((r * steps, D_INNER), lambda i: (i, 0)), cst_spec, h_spec],
        out_shape=[
            jax.ShapeDtypeStruct((nreq * steps, D_INNER), F32),
            jax.ShapeDtypeStruct(conv_state.shape, F32),
            jax.ShapeDtypeStruct(h0.shape, F32),
        ],
        scratch_shapes=[
            pltpu.VMEM((2, SSM_SLABS, SSD_SROWS, LANES), F32),
            pltpu.VMEM((2, SSD_ST, DT_PAD), F32),
            pltpu.VMEM((2, SSD_ST, D_INNER), F32),
        ],
        compiler_params=_params(1),
        name="ssd_sample",
    )(proj2d, proj2d, proj2d, dt2d, conv_state, h0, conv_w, conv_b, a_log, d_exp, norm_g)


def _tail_sample_kernel(x_ref, ca_ref, yn_ref, gate_ref, wpw_ref, bpw_ref, wssm_ref, wout_ref, gffn_ref,
                        wup_ref, wdown_ref, gfin_ref, y_ref):
    branch_a = _dot(ca_ref[...], wpw_ref[...]) + bpw_ref[...]
    branch_b = _dot(yn_ref[...].astype(BF16), wssm_ref[...])
    merged = (_sigmoid(gate_ref[:, 0:D_MODEL]) * branch_a
              + _sigmoid(gate_ref[:, D_MODEL:2 * D_MODEL]) * branch_b)
    h = x_ref[...] + _dot(merged.astype(BF16), wout_ref[...])
    y_ref[...] = _mlp_final(h, gffn_ref, wup_ref, wdown_ref, gfin_ref)


def _tail_sample(x2d, ca, yn, proj2d, w_pw, b_pw, w_ssm, w_out, g_ffn, w_up, w_down, g_final):
    m = x2d.shape[0]
    vec = _const_spec((1, D_MODEL))
    return pl.pallas_call(
        _tail_sample_kernel,
        grid=(1,),
        in_specs=[
            pl.BlockSpec((m, D_MODEL), lambda i: (0, 0)),
            pl.BlockSpec((m, C_CONV), lambda i: (0, 0)),
            pl.BlockSpec((m, D_INNER), lambda i: (0, 0)),
            pl.BlockSpec((m, 2 * D_MODEL), lambda i: (0, 1)),
            _const_spec((C_CONV, D_MODEL)), vec, _const_spec((D_INNER, D_MODEL)),
            _const_spec((D_MODEL, D_MODEL)), vec, _const_spec((D_MODEL, D_FF)),
            _const_spec((D_FF, D_MODEL)), vec,
        ],
        out_specs=pl.BlockSpec((m, D_MODEL), lambda i: (0, 0)),
        out_shape=jax.ShapeDtypeStruct((m, D_MODEL), F32),
        compiler_params=_params(1),
        name="tail_sample",
    )(x2d, ca, yn, proj2d, w_pw, b_pw, w_ssm, w_out, g_ffn, w_up, w_down, g_final)


def kernel(x_prompt, x_sample, state_conf_conv, state_ssm_conv, state_ssm, g_mix, w_in, conf_dw_w,
           conf_dw_b, conf_ln_g, conf_ln_b, conf_w_pw, conf_b_pw, ssm_conv_w, ssm_conv_b, ssm_dt_bias,
           ssm_a_log, ssm_d, ssm_norm_g, ssm_w_out, w_out, g_ffn, w_up, w_down, g_final):
    depth = w_in.shape[0]
    assert depth == 1
    nb, seq, _ = x_prompt.shape
    nreq, steps, _ = x_sample.shape
    i = 0

    def row(v):
        return v.reshape(1, -1)

    def pad_lanes(v, width):
        return jnp.pad(v, ((0, 0), (0, width - v.shape[1])))

    gm = row(g_mix[i])
    w_in_t = w_in[i].T
    w_dt = pad_lanes(w_in_t[MAIN_COLS:, :].T, DT_PAD)
    b_dt = pad_lanes(row(ssm_dt_bias[i]), DT_PAD)
    a_log = pad_lanes(row(ssm_a_log[i]), DT_PAD)
    d_exp = row(jnp.repeat(ssm_d[i], HEADDIM))
    norm_g = row(ssm_norm_g[i])
    dw_w = conf_dw_w[i]
    dw_b, ln_g, ln_b = row(conf_dw_b[i]), row(conf_ln_g[i]), row(conf_ln_b[i])
    cw, cb = ssm_conv_w[i], row(ssm_conv_b[i])
    w_pw, b_pw = conf_w_pw[i].astype(BF16), row(conf_b_pw[i])
    w_ssm = ssm_w_out[i].astype(BF16)
    mlp_w = (w_out[i].astype(BF16), row(g_ffn[i]), w_up[i].astype(BF16), w_down[i].astype(BF16),
             row(g_final))

    xs = x_sample.reshape(nreq * steps, D_MODEL)
    proj_s, dt_s, w_main = _inproj(xs, gm, w_in_t, w_dt, b_dt)

    ga, conf_p = _conva_prompt(x_prompt, gm, w_main, dw_w, dw_b, ln_g, ln_b, w_pw, b_pw)
    head_row = jnp.arange(3 * DT_PAD, dtype=jnp.int32)[:, None] % DT_PAD
    ex = (head_row == jnp.arange(D_INNER, dtype=jnp.int32)[None, :] // HEADDIM).astype(BF16)
    gb, scv_p, h_p = _ssd_prompt(x_prompt, gm, w_main, w_dt, b_dt, w_ssm, cw, cb, a_log, d_exp, norm_g, ex)
    m_p = nb * seq
    y_p = _tail_prompt(x_prompt.reshape(m_p, D_MODEL), ga.reshape(m_p, D_MODEL), gb.reshape(m_p, D_MODEL),
                       *mlp_w)

    planes = (0, 2, 1, 3)
    ca_s, conf_s = _conva_sample(proj_s, state_conf_conv.transpose(planes), steps, dw_w, dw_b, ln_g, ln_b)
    yn_s, scv_s, h_s = _ssd_sample(proj_s, dt_s, state_ssm_conv.transpose(planes), state_ssm, steps, cw, cb,
                                   a_log, d_exp, norm_g)
    y_s = _tail_sample(xs, ca_s, yn_s, proj_s, w_pw, b_pw, w_ssm, *mlp_w)

    return (y_p.reshape(nb, seq, D_MODEL), y_s.reshape(nreq, steps, D_MODEL),
            conf_p[None], scv_p[None], h_p[None], conf_s.transpose(planes), scv_s.transpose(planes), h_s)
```

```python
import jax
import jax.numpy as jnp
from jax import lax
from jax.experimental import pallas as pl
from jax.experimental.pallas import tpu as pltpu

F32 = jnp.float32
BF16 = jnp.bfloat16

D_MODEL = 1024
C_CONV = 1024
CONF_KERNEL = 31
CONF_HALO = CONF_KERNEL - 1
D_INNER = 2048
HEADDIM = 64
N_HEADS = 32
N_GROUPS = 8
HEADS_PER_GROUP = 4
GROUP_W = HEADS_PER_GROUP * HEADDIM
D_STATE = 128
SSM_CONV = 4
SSM_HALO = SSM_CONV - 1
CONV_DIM = 4096
CHUNK = 128
D_FF = 4096
EPS = 1e-6
MAIN_COLS = 10240
DT_PAD = 128
LANES = 128
CONF_SLABS = C_CONV // LANES
SSM_SLABS = CONV_DIM // LANES
X_SLABS = D_INNER // LANES
VMEM_LIMIT = 56 * 1024 * 1024


def _sigmoid(x):
    return jax.nn.sigmoid(x)


def _silu(x):
    return x * jax.nn.sigmoid(x)


def _softplus(x):
    return jnp.maximum(x, 0.0) + jnp.log1p(jnp.exp(-jnp.abs(x)))


def _rmsnorm(x, g):
    return x * lax.rsqrt(jnp.mean(x * x, axis=-1, keepdims=True) + EPS) * g


def _dot(a, b):
    return jnp.dot(a, b, preferred_element_type=F32)


def _dot_nt(a, b):
    return lax.dot_general(a, b, (((1,), (1,)), ((), ())), preferred_element_type=F32)


def _dot_tn(a, b):
    return lax.dot_general(a, b, (((0,), (0,)), ((), ())), preferred_element_type=F32)


def _split3(x):
    hi = x.astype(BF16)
    r1 = x - hi.astype(F32)
    mid = r1.astype(BF16)
    lo = (r1 - mid.astype(F32)).astype(BF16)
    return hi, mid, lo


def _cumsum_rows(a, tri):
    hi, mid, lo = _split3(a)
    return _dot(tri, hi) + _dot(tri, mid) + _dot(tri, lo)


def _lanes(j):
    return slice(j * LANES, (j + 1) * LANES)


def _const_spec(shape, index=None):
    nd = len(shape)
    idx = index if index is not None else (0,) * nd
    return pl.BlockSpec(shape, lambda *_: idx, pipeline_mode=pl.Buffered(1))


def _params(ngrid):
    return pltpu.CompilerParams(dimension_semantics=("arbitrary",) * ngrid, vmem_limit_bytes=VMEM_LIMIT)


def _ln_swish(y, g, b):
    mu = jnp.mean(y, axis=-1, keepdims=True)
    yc = y - mu
    yn = yc * lax.rsqrt(jnp.mean(yc * yc, axis=-1, keepdims=True) + EPS) * g + b
    return _silu(yn)


def _mlp_final(h, gffn_ref, wup_ref, wdown_ref, gfin_ref):
    hidden = jnp.square(jnp.maximum(_dot(_rmsnorm(h, gffn_ref[...]).astype(BF16), wup_ref[...]), 0.0))
    h = h + _dot(hidden.astype(BF16), wdown_ref[...])
    return _rmsnorm(h, gfin_ref[...])


CONVA_T = 512
CONVA_RB = 64
CONVA_PAD = 32


def _conva_prompt_kernel(x_ref, gmix_ref, wa_ref, wb_ref, wga_ref, dw_ref, dwb_ref, lng_ref, lnb_ref,
                         wpw_ref, bpw_ref, ga_ref, st_ref, xe, conv, u_scr):
    t = CONVA_T

    @pl.when(pl.program_id(1) == 0)
    def _():
        xe[:, 0:CONVA_PAD, :] = jnp.zeros((CONF_SLABS, CONVA_PAD, LANES), F32)

    u_scr[...] = _rmsnorm(x_ref[0], gmix_ref[...]).astype(BF16)
    off = CONVA_PAD - CONF_HALO
    glu = _dot(u_scr[...], wa_ref[...]) * _sigmoid(_dot(u_scr[...], wb_ref[...]))
    for j in range(CONF_SLABS):
        xe[j, CONVA_PAD:CONVA_PAD + t, :] = glu[:, _lanes(j)]
    for j in range(CONF_SLABS):
        for rb in range(t // CONVA_RB):
            r0 = rb * CONVA_RB
            acc = jnp.broadcast_to(dwb_ref[:, _lanes(j)], (CONVA_RB, LANES))
            for k in range(CONF_KERNEL):
                acc = acc + dw_ref[k:k + 1, _lanes(j)] * xe[j, r0 + off + k:r0 + off + k + CONVA_RB, :]
            conv[r0:r0 + CONVA_RB, _lanes(j)] = acc
    ca = _ln_swish(conv[...], lng_ref[...], lnb_ref[...]).astype(BF16)
    ga_ref[0] = _sigmoid(_dot(u_scr[...], wga_ref[...])) * (_dot(ca, wpw_ref[...]) + bpw_ref[...])
    for j in range(CONF_SLABS):
        st_ref[0, :, _lanes(j)] = xe[j, t + off:t + CONVA_PAD, :]
        xe[j, 0:CONVA_PAD, :] = xe[j, t:t + CONVA_PAD, :]


def _conva_prompt(x3, g_mix, w_main, dw_w, dw_b, ln_g, ln_b, w_pw, b_pw):
    nb, seq, _ = x3.shape
    t = CONVA_T
    vec = _const_spec((1, C_CONV))
    sq = (D_MODEL, D_MODEL)
    return pl.pallas_call(
        _conva_prompt_kernel,
        grid=(nb, seq // t),
        in_specs=[
            pl.BlockSpec((1, t, D_MODEL), lambda b, i: (b, i, 0)),
            vec,
            _const_spec(sq, (0, 0)), _const_spec(sq, (0, 1)), _const_spec(sq, (0, 2)),
            _const_spec((CONF_KERNEL, C_CONV)), vec, vec, vec,
            _const_spec(sq), vec,
        ],
        out_specs=[
            pl.BlockSpec((1, t, D_MODEL), lambda b, i: (b, i, 0)),
            pl.BlockSpec((1, CONF_HALO, C_CONV), lambda b, i: (b, 0, 0)),
        ],
        out_shape=[
            jax.ShapeDtypeStruct((nb, seq, D_MODEL), F32),
            jax.ShapeDtypeStruct((nb, CONF_HALO, C_CONV), F32),
        ],
        scratch_shapes=[
            pltpu.VMEM((CONF_SLABS, t + CONVA_PAD, LANES), F32),
            pltpu.VMEM((t, C_CONV), F32),
            pltpu.VMEM((t, D_MODEL), BF16),
        ],
        compiler_params=_params(2),
        name="conva_prompt",
    )(x3, g_mix, w_main, w_main, w_main, dw_w, dw_b, ln_g, ln_b, w_pw, b_pw)


def _expand_heads(v, g, lo_half):
    rows = v.shape[0]

    def col(e):
        return jnp.broadcast_to(v[:, 4 * g + e:4 * g + e + 1], (rows, LANES))

    return jnp.concatenate([jnp.where(lo_half, col(0), col(1)), jnp.where(lo_half, col(2), col(3))], axis=1)


def _gated_group_norm(y, z, g):
    y = y * _silu(z)
    return y * lax.rsqrt(jnp.mean(y * y, axis=-1, keepdims=True) + EPS) * g


LOG2E = 1.4426950408889634


def _cat3(v):
    return jnp.concatenate(_split3(v), axis=1)


def _ssd_chunk(rows, dt, xs_scr, bc_scr, z_scr, yn_scr, h_ref, a_neg, dexp_ref, ng_ref, ex_ref, acumt_scr):
    t = CHUNK
    r_i = lax.broadcasted_iota(jnp.int32, (t, t), 0)
    c_i = lax.broadcasted_iota(jnp.int32, (t, t), 1)
    causal = r_i >= c_i
    tri = causal.astype(BF16)
    lo_half = lax.broadcasted_iota(jnp.int32, (t, LANES), 1) < HEADDIM
    zeros = jnp.zeros((t, LANES), F32)
    zeros_b = jnp.zeros((t, t), BF16)

    acum = _cumsum_rows(dt * a_neg, tri)
    a_last = acum[t - 1:t, :]
    ealast = jnp.exp(a_last)
    acum2 = acum * LOG2E
    acumt_scr[...] = acum2.T
    dt3 = _cat3(dt)

    def stage_a(g):
        gs = slice(g * GROUP_W, (g + 1) * GROUP_W)
        xs = xs_scr[rows, gs]
        bm = bc_scr[rows, _lanes(g)]
        cm = bc_scr[rows, _lanes(N_GROUPS + g)]
        scores = _dot_nt(cm, bm)
        h_g = h_ref[0, 4 * g:4 * g + 4].reshape(GROUP_W, D_STATE)
        colb = [jnp.broadcast_to(acum2[:, 4 * g + e:4 * g + e + 1], (t, LANES))
                for e in range(HEADS_PER_GROUP)]
        acx = jnp.concatenate([jnp.where(lo_half, colb[0], colb[1]),
                               jnp.where(lo_half, colb[2], colb[3])], axis=1)
        xdt = xs * _dot(dt3, ex_ref[:, gs])
        y = _dot_nt(cm, h_g.astype(BF16)) * jnp.exp2(acx) + xs * dexp_ref[:, gs]
        ms, xblk = [], []
        for e in range(HEADS_PER_GROUP):
            hd = 4 * g + e
            seg = colb[e] - acumt_scr[hd:hd + 1, :]
            m = (scores * jnp.exp2(seg)).astype(BF16)
            ms.append(jnp.where(causal, m, zeros_b))
            half = xdt[:, _lanes(e // 2)]
            keep = jnp.where(lo_half, half, 0.0) if e % 2 == 0 else jnp.where(lo_half, 0.0, half)
            blk = jnp.concatenate([keep, zeros] if e < 2 else [zeros, keep], axis=1)
            xblk.append(blk.astype(BF16))
        xw = (xdt * jnp.exp2(acx[t - 1:t, :] - acx)).astype(BF16)
        return y, jnp.concatenate(ms, axis=1), jnp.concatenate(xblk, axis=0), xw, bm

    def stage_b(g, staged):
        y, mcat, xcat, xw, bm = staged
        gs = slice(g * GROUP_W, (g + 1) * GROUP_W)
        y = y + _dot(mcat, xcat)
        upd = _dot_tn(xw, bm)
        for e in range(HEADS_PER_GROUP):
            hd = 4 * g + e
            h_ref[0, hd] = h_ref[0, hd] * ealast[:, hd:hd + 1] + upd[e * HEADDIM:(e + 1) * HEADDIM, :]
        yn_scr[rows, gs] = _gated_group_norm(y, z_scr[rows, gs], ng_ref[:, gs]).astype(BF16)

    staged = stage_a(0)
    for g in range(N_GROUPS):
        following = stage_a(g + 1) if g + 1 < N_GROUPS else None
        stage_b(g, staged)
        staged = following


SSDP_T = 256
SSDP_RB = 64
SSD_PAD = 8


def _ssd_prompt_kernel(x_ref, gmix_ref, wz_ref, wx_ref, wbc_ref, wdt_ref, bdt_ref, wgb_ref, wssm_ref,
                       cw_ref, cb_ref, alog_ref, dexp_ref, ng_ref, ex_ref,
                       gb_ref, ncv_ref, h_ref,
                       xe, u_scr, xs_scr, bc_scr, z_scr, yn_scr, acumt_scr):
    t = SSDP_T

    @pl.when(pl.program_id(1) == 0)
    def _():
        xe[:, 0:SSD_PAD, :] = jnp.zeros((SSM_SLABS, SSD_PAD, LANES), F32)
        h_ref[...] = jnp.zeros(h_ref.shape, F32)

    u_scr[...] = _rmsnorm(x_ref[0], gmix_ref[...]).astype(BF16)
    a_neg = -jnp.exp(alog_ref[...])
    off = SSD_PAD - SSM_HALO
    for c in range(t // CHUNK):
        c0 = c * CHUNK
        rows = slice(c0, c0 + CHUNK)
        u_c = u_scr[rows, :]
        z_scr[rows, :] = _dot(u_c, wz_ref[...])
        dt = _softplus(_dot(u_c, wdt_ref[...].astype(BF16)) + bdt_ref[...])
        xr = _dot(u_c, wx_ref[...])
        for j in range(X_SLABS):
            xe[j, SSD_PAD + c0:SSD_PAD + c0 + CHUNK, :] = xr[:, _lanes(j)]
        bcr = _dot(u_c, wbc_ref[...])
        for j in range(X_SLABS):
            xe[X_SLABS + j, SSD_PAD + c0:SSD_PAD + c0 + CHUNK, :] = bcr[:, _lanes(j)]
        for j in range(SSM_SLABS):
            for rb in range(CHUNK // SSDP_RB):
                r0 = c0 + rb * SSDP_RB
                acc = jnp.broadcast_to(cb_ref[:, _lanes(j)], (SSDP_RB, LANES))
                for k in range(SSM_CONV):
                    acc = acc + cw_ref[k:k + 1, _lanes(j)] * xe[j, r0 + off + k:r0 + off + k + SSDP_RB, :]
                if j < X_SLABS:
                    xs_scr[r0:r0 + SSDP_RB, _lanes(j)] = _silu(acc)
                else:
                    bc_scr[r0:r0 + SSDP_RB, _lanes(j - X_SLABS)] = _silu(acc).astype(BF16)
        _ssd_chunk(rows, dt, xs_scr, bc_scr, z_scr, yn_scr, h_ref, a_neg, dexp_ref, ng_ref, ex_ref, acumt_scr)
        gb_ref[0, rows, :] = _sigmoid(_dot(u_c, wgb_ref[...])) * _dot(yn_scr[rows, :], wssm_ref[...])
    for j in range(SSM_SLABS):
        ncv_ref[0, :, _lanes(j)] = xe[j, t + off:t + SSD_PAD, :]
        xe[j, 0:SSD_PAD, :] = xe[j, t:t + SSD_PAD, :]


def _ssd_prompt(x3, g_mix, w_main, w_dt, b_dt, w_ssm, conv_w, conv_b, a_log, d_exp, norm_g, ex):
    nb, seq, _ = x3.shape
    t = SSDP_T
    wide = (D_MODEL, D_INNER)
    return pl.pallas_call(
        _ssd_prompt_kernel,
        grid=(nb, seq // t),
        in_specs=[
            pl.BlockSpec((1, t, D_MODEL), lambda b, i: (b, i, 0)),
            _const_spec((1, D_MODEL)),
            _const_spec(wide, (0, 2)), _const_spec(wide, (0, 3)), _const_spec(wide, (0, 4)),
            _const_spec((D_MODEL, DT_PAD)), _const_spec((1, DT_PAD)),
            _const_spec((D_MODEL, D_MODEL), (0, 3)),
            _const_spec((D_INNER, D_MODEL)),
            _const_spec((SSM_CONV, CONV_DIM)), _const_spec((1, CONV_DIM)), _const_spec((1, DT_PAD)),
            _const_spec((1, D_INNER)), _const_spec((1, D_INNER)),
            _const_spec((3 * DT_PAD, D_INNER)),
        ],
        out_specs=[
            pl.BlockSpec((1, t, D_MODEL), lambda b, i: (b, i, 0)),
            pl.BlockSpec((1, SSM_HALO, CONV_DIM), lambda b, i: (b, 0, 0)),
            pl.BlockSpec((1, N_HEADS, HEADDIM, D_STATE), lambda b, i: (b, 0, 0, 0)),
        ],
        out_shape=[
            jax.ShapeDtypeStruct((nb, seq, D_MODEL), F32),
            jax.ShapeDtypeStruct((nb, SSM_HALO, CONV_DIM), F32),
            jax.ShapeDtypeStruct((nb, N_HEADS, HEADDIM, D_STATE), F32),
        ],
        scratch_shapes=[
            pltpu.VMEM((SSM_SLABS, t + SSD_PAD, LANES), F32),
            pltpu.VMEM((t, D_MODEL), BF16),
            pltpu.VMEM((t, D_INNER), F32),
            pltpu.VMEM((t, D_INNER), BF16),
            pltpu.VMEM((t, D_INNER), F32),
            pltpu.VMEM((t, D_INNER), BF16),
            pltpu.VMEM((DT_PAD, CHUNK), F32),
        ],
        compiler_params=_params(2),
        name="ssd_prompt",
    )(x3, g_mix, w_main, w_main, w_main, w_dt, b_dt, w_main, w_ssm, conv_w, conv_b, a_log, d_exp, norm_g,
      ex)


def _tail_prompt_kernel(x_ref, ga_ref, gb_ref, wout_ref, gffn_ref, wup_ref, wdown_ref, gfin_ref, y_ref):
    merged = (ga_ref[...] + gb_ref[...]).astype(BF16)
    h = x_ref[...] + _dot(merged, wout_ref[...])
    y_ref[...] = _mlp_final(h, gffn_ref, wup_ref, wdown_ref, gfin_ref)


def _tail_prompt(x2d, ga, gb, w_out, g_ffn, w_up, w_down, g_final, tm=512):
    m = x2d.shape[0]
    tile = pl.BlockSpec((tm, D_MODEL), lambda i: (i, 0))
    vec = _const_spec((1, D_MODEL))
    return pl.pallas_call(
        _tail_prompt_kernel,
        grid=(m // tm,),
        in_specs=[tile, tile, tile, _const_spec((D_MODEL, D_MODEL)), vec,
                  _const_spec((D_MODEL, D_FF)), _const_spec((D_FF, D_MODEL)), vec],
        out_specs=tile,
        out_shape=jax.ShapeDtypeStruct((m, D_MODEL), F32),
        compiler_params=_params(1),
        name="tail_prompt",
    )(x2d, ga, gb, w_out, g_ffn, w_up, w_down, g_final)


def _inproj_kernel(x_ref, g_ref, w_ref, wdt_ref, bdt_ref, proj_ref, dt_ref, wb_ref, u_scr):
    @pl.when(pl.program_id(0) == 0)
    def _():
        ub = _rmsnorm(x_ref[...], g_ref[...]).astype(BF16)
        u_scr[...] = ub
        dt_ref[...] = _softplus(_dot(ub, wdt_ref[...].astype(BF16)) + bdt_ref[...])

    wb = w_ref[0].astype(BF16)
    wb_ref[...] = wb
    proj_ref[...] = _dot(u_scr[...], wb)


def _inproj(x2d, g_mix, w_in, w_dt, b_dt, tn=1024):
    m = x2d.shape[0]
    return pl.pallas_call(
        _inproj_kernel,
        grid=(MAIN_COLS // tn,),
        in_specs=[
            pl.BlockSpec((m, D_MODEL), lambda j: (0, 0)),
            pl.BlockSpec((1, D_MODEL), lambda j: (0, 0)),
            pl.BlockSpec((1, D_MODEL, tn), lambda j: (0, 0, j)),
            pl.BlockSpec((D_MODEL, DT_PAD), lambda j: (0, 0)),
            pl.BlockSpec((1, DT_PAD), lambda j: (0, 0)),
        ],
        out_specs=[
            pl.BlockSpec((m, tn), lambda j: (0, j)),
            pl.BlockSpec((m, DT_PAD), lambda j: (0, 0)),
            pl.BlockSpec((D_MODEL, tn), lambda j: (0, j)),
        ],
        out_shape=[
            jax.ShapeDtypeStruct((m, MAIN_COLS), F32),
            jax.ShapeDtypeStruct((m, DT_PAD), F32),
            jax.ShapeDtypeStruct((D_MODEL, MAIN_COLS), BF16),
        ],
        scratch_shapes=[pltpu.VMEM((m, D_MODEL), BF16)],
        compiler_params=_params(1),
        name="inproj_sample",
    )(x2d, g_mix, w_in, w_dt, b_dt)


CONVA_SR = 32


def _conva_sample_kernel(a_ref, b_ref, st_ref, w_ref, bias_ref, lng_ref, lnb_ref,
                         ca_ref, nst_ref, g3, c3):
    nreq = st_ref.shape[2]
    steps = a_ref.shape[0] // nreq
    glu = a_ref[...] * _sigmoid(b_ref[...])
    for j in range(CONF_SLABS):
        g3[j] = glu[:, _lanes(j)]

    def plane(h, j):
        if h < CONF_HALO:
            return st_ref[0, h, :, _lanes(j)]
        return g3[j, pl.ds(h - CONF_HALO, nreq, stride=steps), :]

    for j in range(CONF_SLABS):
        for t in range(steps):
            acc = jnp.broadcast_to(bias_ref[:, _lanes(j)], (nreq, LANES))
            for k in range(CONF_KERNEL):
                acc = acc + w_ref[k:k + 1, _lanes(j)] * plane(t + k, j)
            c3[j, pl.ds(t, nreq, stride=steps), :] = acc
        for h in range(CONF_HALO):
            nst_ref[0, h, :, _lanes(j)] = plane(h + steps, j)
    conv = jnp.concatenate([c3[j] for j in range(CONF_SLABS)], axis=1)
    ca_ref[...] = _ln_swish(conv, lng_ref[...], lnb_ref[...]).astype(BF16)


def _conva_sample(proj2d, state_planes, steps, dw_w, dw_b, ln_g, ln_b):
    depth, _, nreq, _ = state_planes.shape
    r = CONVA_SR
    vec = pl.BlockSpec((1, C_CONV), lambda i: (0, 0))
    st_spec = pl.BlockSpec((depth, CONF_HALO, r, C_CONV), lambda i: (0, 0, i, 0))
    slab = pltpu.VMEM((CONF_SLABS, r * steps, LANES), F32)
    return pl.pallas_call(
        _conva_sample_kernel,
        grid=(nreq // r,),
        in_specs=[
            pl.BlockSpec((r * steps, C_CONV), lambda i: (i, 0)),
            pl.BlockSpec((r * steps, C_CONV), lambda i: (i, 1)),
            st_spec,
            pl.BlockSpec((CONF_KERNEL, C_CONV), lambda i: (0, 0)),
            vec, vec, vec,
        ],
        out_specs=[pl.BlockSpec((r * steps, C_CONV), lambda i: (i, 0)), st_spec],
        out_shape=[
            jax.ShapeDtypeStruct((nreq * steps, C_CONV), BF16),
            jax.ShapeDtypeStruct(state_planes.shape, F32),
        ],
        scratch_shapes=[slab, slab],
        compiler_params=_params(1),
        name="conva_sample",
    )(proj2d, proj2d, state_planes, dw_w, dw_b, ln_g, ln_b)


SSD_SR = 8
SSD_ST = 8
SSD_SROWS = 16


def _ssd_sample_kernel(z_ref, xr_ref, bcr_ref, dt_ref, cst_ref, h0_ref, cw_ref, cb_ref, alog_ref,
                       dexp_ref, ng_ref, yn_ref, ncv_ref, h_ref, xe_all, dt8, z8):
    nreq = cst_ref.shape[2]
    steps = z_ref.shape[0] // nreq
    t = SSD_ST
    xe_all[...] = jnp.zeros(xe_all.shape, F32)
    dt8[...] = jnp.zeros(dt8.shape, F32)
    z8[...] = jnp.zeros(z8.shape, F32)
    rows = lax.broadcasted_iota(jnp.int32, (t, 1), 0)
    live = rows < steps
    lo_half = lax.broadcasted_iota(jnp.int32, (t, LANES), 1) < HEADDIM
    a_neg = -jnp.exp(alog_ref[...])
    groups = range(N_GROUPS)

    def request_phases(r, slot):
        rr = slice(r * steps, (r + 1) * steps)
        xs_ = xe_all.at[slot]
        v = {}

        def conv(j):
            acc = jnp.broadcast_to(cb_ref[:, _lanes(j)], (t, LANES))
            for k in range(SSM_CONV):
                acc = acc + cw_ref[k:k + 1, _lanes(j)] * xs_[j, k:k + t, :]
            return jnp.where(live, _silu(acc), 0.0)

        def p_load():
            xr = xr_ref[rr, :]
            bcr = bcr_ref[rr, :]
            for j in range(SSM_SLABS):
                for h in range(SSM_HALO):
                    xs_[j, h:h + 1, :] = cst_ref[0, h, r:r + 1, _lanes(j)]
                new = xr[:, _lanes(j)] if j < X_SLABS else bcr[:, _lanes(j - X_SLABS)]
                xs_[j, SSM_HALO:SSM_HALO + steps, :] = new
                for h in range(SSM_HALO):
                    ncv_ref[0, h, r:r + 1, _lanes(j)] = xs_[j, steps + h:steps + h + 1, :]
            dt8[slot, 0:steps, :] = dt_ref[rr, :]
            z8[slot, 0:steps, :] = z_ref[rr, :]

        def p_conv():
            dt = dt8[slot]
            a = dt * a_neg
            acum = jnp.zeros((t, DT_PAD), F32)
            for s in range(steps):
                acum = acum + jnp.where(rows >= s, a[s:s + 1, :], 0.0)
            tiles = [conv(j) for j in range(SSM_SLABS)]
            v.update(dt=dt, acum=acum, ealast=jnp.exp(acum[t - 1:t, :]),
                     xs=[jnp.concatenate([tiles[2 * g], tiles[2 * g + 1]], axis=1) for g in groups],
                     bm=[tiles[X_SLABS + g] for g in groups],
                     cm=[tiles[X_SLABS + N_GROUPS + g] for g in groups])

        def p_small_products():
            v["scores"] = [_dot_nt(v["cm"][g], v["bm"][g]) for g in groups]
            v["yoff"] = [_dot_nt(v["cm"][g], h0_ref[0, r, 4 * g:4 * g + 4].reshape(GROUP_W, D_STATE))
                         for g in groups]

        def p_spread():
            v["acx"] = [_expand_heads(v["acum"], g, lo_half) for g in groups]
            v["xdt"] = [v["xs"][g] * _expand_heads(v["dt"], g, lo_half) for g in groups]

        def p_vector():
            ys = []
            for g in groups:
                gs = slice(g * GROUP_W, (g + 1) * GROUP_W)
                acx, xdt, scores = v["acx"][g], v["xdt"][g], v["scores"][g]
                y = v["yoff"][g] * jnp.exp(acx) + v["xs"][g] * dexp_ref[:, gs]
                for s in range(steps):
                    decay = jnp.exp(jnp.where(rows >= s, acx - acx[s:s + 1, :], -jnp.inf))
                    y = y + (scores[:, s:s + 1] * decay) * xdt[s:s + 1, :]
                ys.append(y)
            v["ys"] = ys

        def p_state_products():
            v["upd"] = [_dot_tn(v["xdt"][g] * jnp.exp(v["acx"][g][t - 1:t, :] - v["acx"][g]), v["bm"][g])
                        for g in groups]

        def p_state_store():
            for g in groups:
                for e in range(HEADS_PER_GROUP):
                    hd = 4 * g + e
                    h_ref[0, r, hd] = (h0_ref[0, r, hd] * v["ealast"][:, hd:hd + 1]
                                       + v["upd"][g][e * HEADDIM:(e + 1) * HEADDIM, :])

        def p_out():
            for g in groups:
                gs = slice(g * GROUP_W, (g + 1) * GROUP_W)
                yn = _gated_group_norm(v["ys"][g], z8[slot, :, gs], ng_ref[:, gs])
                yn_ref[rr, gs] = yn[0:steps, :]

        return [p_load, p_conv, p_small_products, p_spread, p_vector, p_state_products, p_state_store, p_out]

    for pair in range(nreq // 2):
        for pa, pb in zip(request_phases(2 * pair, 0), request_phases(2 * pair + 1, 1)):
            pa()
            pb()


def _ssd_sample(proj2d, dt2d, conv_state, h0, steps, conv_w, conv_b, a_log, d_exp, norm_g):
    depth, nreq = conv_state.shape[0], conv_state.shape[2]
    r = SSD_SR
    cst_spec = pl.BlockSpec((depth, SSM_HALO, r, CONV_DIM), lambda i: (0, 0, i, 0))
    h_spec = pl.BlockSpec((depth, r, N_HEADS, HEADDIM, D_STATE), lambda i: (0, i, 0, 0, 0))
    return pl.pallas_call(
        _ssd_sample_kernel,
        grid=(nreq // r,),
        in_specs=[
            pl.BlockSpec((r * steps, D_INNER), lambda i: (i, 2)),
            pl.BlockSpec((r * steps, D_INNER), lambda i: (i, 3)),
            pl.BlockSpec((r * steps, D_INNER), lambda i: (i, 4)),
            pl.BlockSpec((r * steps, DT_PAD), lambda i: (i, 0)),
            cst_spec,
            h_spec,
            pl.BlockSpec((SSM_CONV, CONV_DIM), lambda i: (0, 0)),
            pl.BlockSpec((1, CONV_DIM), lambda i: (0, 0)),
            pl.BlockSpec((1, DT_PAD), lambda i: (0, 0)),
            pl.BlockSpec((1, D_INNER), lambda i: (0, 0)),
            pl.BlockSpec((1, D_INNER), lambda i: (0, 0)),
        ],
        out_specs=[pl.BlockSpec((r * steps, D_INNER), lambda i: (i, 0)), cst_spec, h_spec],
        out_shape=[
            jax.ShapeDtypeStruct((nreq * steps, D_INNER), F32),
            jax.ShapeDtypeStruct(conv_state.shape, F32),
            jax.ShapeDtypeStruct(h0.shape, F32),
        ],
        scratch_shapes=[
            pltpu.VMEM((2, SSM_SLABS, SSD_SROWS, LANES), F32),
            pltpu.VMEM((2, SSD_ST, DT_PAD), F32),
            pltpu.VMEM((2, SSD_ST, D_INNER), F32),
        ],
        compiler_params=_params(1),
        name="ssd_sample",
    )(proj2d, proj2d, proj2d, dt2d, conv_state, h0, conv_w, conv_b, a_log, d_exp, norm_g)


def _tail_sample_kernel(x_ref, ca_ref, yn_ref, gate_ref, wpw_ref, bpw_ref, wssm_ref, wout_ref, gffn_ref,
                        wup_ref, wdown_ref, gfin_ref, y_ref):
    branch_a = _dot(ca_ref[...], wpw_ref[...]) + bpw_ref[...]
    branch_b = _dot(yn_ref[...].astype(BF16), wssm_ref[...])
    merged = (_sigmoid(gate_ref[:, 0:D_MODEL]) * branch_a
              + _sigmoid(gate_ref[:, D_MODEL:2 * D_MODEL]) * branch_b)
    h = x_ref[...] + _dot(merged.astype(BF16), wout_ref[...])
    y_ref[...] = _mlp_final(h, gffn_ref, wup_ref, wdown_ref, gfin_ref)


def _tail_sample(x2d, ca, yn, proj2d, w_pw, b_pw, w_ssm, w_out, g_ffn, w_up, w_down, g_final):
    m = x2d.shape[0]
    vec = _const_spec((1, D_MODEL))
    return pl.pallas_call(
        _tail_sample_kernel,
        grid=(1,),
        in_specs=[
            pl.BlockSpec((m, D_MODEL), lambda i: (0, 0)),
            pl.BlockSpec((m, C_CONV), lambda i: (0, 0)),
            pl.BlockSpec((m, D_INNER), lambda i: (0, 0)),
            pl.BlockSpec((m, 2 * D_MODEL), lambda i: (0, 1)),
            _const_spec((C_CONV, D_MODEL)), vec, _const_spec((D_INNER, D_MODEL)),
            _const_spec((D_MODEL, D_MODEL)), vec, _const_spec((D_MODEL, D_FF)),
            _const_spec((D_FF, D_MODEL)), vec,
        ],
        out_specs=pl.BlockSpec((m, D_MODEL), lambda i: (0, 0)),
        out_shape=jax.ShapeDtypeStruct((m, D_MODEL), F32),
        compiler_params=_params(1),
        name="tail_sample",
    )(x2d, ca, yn, proj2d, w_pw, b_pw, w_ssm, w_out, g_ffn, w_up, w_down, g_final)


def kernel(x_prompt, x_sample, state_conf_conv, state_ssm_conv, state_ssm, g_mix, w_in, conf_dw_w,
           conf_dw_b, conf_ln_g, conf_ln_b, conf_w_pw, conf_b_pw, ssm_conv_w, ssm_conv_b, ssm_dt_bias,
           ssm_a_log, ssm_d, ssm_norm_g, ssm_w_out, w_out, g_ffn, w_up, w_down, g_final):
    depth = w_in.shape[0]
    assert depth == 1
    nb, seq, _ = x_prompt.shape
    nreq, steps, _ = x_sample.shape
    i = 0

    def row(v):
        return v.reshape(1, -1)

    def pad_lanes(v, width):
        return jnp.pad(v, ((0, 0), (0, width - v.shape[1])))

    gm = row(g_mix[i])
    w_dt = pad_lanes(w_in[i][:, MAIN_COLS:], DT_PAD)
    b_dt = pad_lanes(row(ssm_dt_bias[i]), DT_PAD)
    a_log = pad_lanes(row(ssm_a_log[i]), DT_PAD)
    d_exp = row(jnp.repeat(ssm_d[i], HEADDIM))
    norm_g = row(ssm_norm_g[i])
    dw_w = conf_dw_w[i]
    dw_b, ln_g, ln_b = row(conf_dw_b[i]), row(conf_ln_g[i]), row(conf_ln_b[i])
    cw, cb = ssm_conv_w[i], row(ssm_conv_b[i])
    w_pw, b_pw = conf_w_pw[i].astype(BF16), row(conf_b_pw[i])
    w_ssm = ssm_w_out[i].astype(BF16)
    mlp_w = (w_out[i].astype(BF16), row(g_ffn[i]), w_up[i].astype(BF16), w_down[i].astype(BF16),
             row(g_final))

    xs = x_sample.reshape(nreq * steps, D_MODEL)
    proj_s, dt_s, w_main = _inproj(xs, gm, w_in, w_dt, b_dt)

    ga, conf_p = _conva_prompt(x_prompt, gm, w_main, dw_w, dw_b, ln_g, ln_b, w_pw, b_pw)
    head_row = jnp.arange(3 * DT_PAD, dtype=jnp.int32)[:, None] % DT_PAD
    ex = (head_row == jnp.arange(D_INNER, dtype=jnp.int32)[None, :] // HEADDIM).astype(BF16)
    gb, scv_p, h_p = _ssd_prompt(x_prompt, gm, w_main, w_dt, b_dt, w_ssm, cw, cb, a_log, d_exp, norm_g, ex)
    m_p = nb * seq
    y_p = _tail_prompt(x_prompt.reshape(m_p, D_MODEL), ga.reshape(m_p, D_MODEL), gb.reshape(m_p, D_MODEL),
                       *mlp_w)

    planes = (0, 2, 1, 3)
    ca_s, conf_s = _conva_sample(proj_s, state_conf_conv.transpose(planes), steps, dw_w, dw_b, ln_g, ln_b)
    yn_s, scv_s, h_s = _ssd_sample(proj_s, dt_s, state_ssm_conv.transpose(planes), state_ssm, steps, cw, cb,
                                   a_log, d_exp, norm_g)
    y_s = _tail_sample(xs, ca_s, yn_s, proj_s, w_pw, b_pw, w_ssm, *mlp_w)

    return (y_p.reshape(nb, seq, D_MODEL), y_s.reshape(nreq, steps, D_MODEL),
            conf_p[None], scv_p[None], h_p[None], conf_s.transpose(planes), scv_s.transpose(planes), h_s)
```

```python
import jax
import jax.numpy as jnp
from jax import lax
from jax.experimental import pallas as pl
from jax.experimental.pallas import tpu as pltpu

F32 = jnp.float32
BF16 = jnp.bfloat16

D_MODEL = 1024
C_CONV = 1024
CONF_KERNEL = 31
CONF_HALO = CONF_KERNEL - 1
D_INNER = 2048
HEADDIM = 64
N_HEADS = 32
N_GROUPS = 8
HEADS_PER_GROUP = 4
GROUP_W = HEADS_PER_GROUP * HEADDIM
D_STATE = 128
SSM_CONV = 4
SSM_HALO = SSM_CONV - 1
CONV_DIM = 4096
CHUNK = 128
D_FF = 4096
EPS = 1e-6
MAIN_COLS = 10240
DT_PAD = 128
LANES = 128
CONF_SLABS = C_CONV // LANES
SSM_SLABS = CONV_DIM // LANES
X_SLABS = D_INNER // LANES
VMEM_LIMIT = 56 * 1024 * 1024


def _sigmoid(x):
    return jax.nn.sigmoid(x)


def _silu(x):
    return x * jax.nn.sigmoid(x)


def _softplus(x):
    return jnp.maximum(x, 0.0) + jnp.log1p(jnp.exp(-jnp.abs(x)))


def _rmsnorm(x, g):
    return x * lax.rsqrt(jnp.mean(x * x, axis=-1, keepdims=True) + EPS) * g


def _dot(a, b):
    return jnp.dot(a, b, preferred_element_type=F32)


def _dot_nt(a, b):
    return lax.dot_general(a, b, (((1,), (1,)), ((), ())), preferred_element_type=F32)


def _dot_tn(a, b):
    return lax.dot_general(a, b, (((0,), (0,)), ((), ())), preferred_element_type=F32)


def _split3(x):
    hi = x.astype(BF16)
    r1 = x - hi.astype(F32)
    mid = r1.astype(BF16)
    lo = (r1 - mid.astype(F32)).astype(BF16)
    return hi, mid, lo


def _cumsum_rows(a, tri):
    hi, mid, lo = _split3(a)
    return _dot(tri, hi) + _dot(tri, mid) + _dot(tri, lo)


def _lanes(j):
    return slice(j * LANES, (j + 1) * LANES)


def _const_spec(shape, index=None):
    nd = len(shape)
    idx = index if index is not None else (0,) * nd
    return pl.BlockSpec(shape, lambda *_: idx, pipeline_mode=pl.Buffered(1))


def _params(ngrid):
    return pltpu.CompilerParams(dimension_semantics=("arbitrary",) * ngrid, vmem_limit_bytes=VMEM_LIMIT)


def _ln_swish(y, g, b):
    mu = jnp.mean(y, axis=-1, keepdims=True)
    yc = y - mu
    yn = yc * lax.rsqrt(jnp.mean(yc * yc, axis=-1, keepdims=True) + EPS) * g + b
    return _silu(yn)


def _mlp_final(h, gffn_ref, wup_ref, wdown_ref, gfin_ref):
    hidden = jnp.square(jnp.maximum(_dot(_rmsnorm(h, gffn_ref[...]).astype(BF16), wup_ref[...]), 0.0))
    h = h + _dot(hidden.astype(BF16), wdown_ref[...])
    return _rmsnorm(h, gfin_ref[...])


CONVA_T = 512
CONVA_RB = 64
CONVA_PAD = 32


def _conva_prompt_kernel(x_ref, gmix_ref, wa_ref, wb_ref, wga_ref, dw_ref, dwb_ref, lng_ref, lnb_ref,
                         wpw_ref, bpw_ref, ga_ref, st_ref, xe, conv, u_scr):
    t = CONVA_T

    @pl.when(pl.program_id(1) == 0)
    def _():
        xe[:, 0:CONVA_PAD, :] = jnp.zeros((CONF_SLABS, CONVA_PAD, LANES), F32)

    u_scr[...] = _rmsnorm(x_ref[0], gmix_ref[...]).astype(BF16)
    off = CONVA_PAD - CONF_HALO
    glu = _dot(u_scr[...], wa_ref[...]) * _sigmoid(_dot(u_scr[...], wb_ref[...]))
    for j in range(CONF_SLABS):
        xe[j, CONVA_PAD:CONVA_PAD + t, :] = glu[:, _lanes(j)]
    for j in range(CONF_SLABS):
        for rb in range(t // CONVA_RB):
            r0 = rb * CONVA_RB
            acc = jnp.broadcast_to(dwb_ref[:, _lanes(j)], (CONVA_RB, LANES))
            for k in range(CONF_KERNEL):
                acc = acc + dw_ref[k:k + 1, _lanes(j)] * xe[j, r0 + off + k:r0 + off + k + CONVA_RB, :]
            conv[r0:r0 + CONVA_RB, _lanes(j)] = acc
    ca = _ln_swish(conv[...], lng_ref[...], lnb_ref[...]).astype(BF16)
    ga_ref[0] = _sigmoid(_dot(u_scr[...], wga_ref[...])) * (_dot(ca, wpw_ref[...]) + bpw_ref[...])
    for j in range(CONF_SLABS):
        st_ref[0, :, _lanes(j)] = xe[j, t + off:t + CONVA_PAD, :]
        xe[j, 0:CONVA_PAD, :] = xe[j, t:t + CONVA_PAD, :]


def _conva_prompt(x3, g_mix, w_main, dw_w, dw_b, ln_g, ln_b, w_pw, b_pw):
    nb, seq, _ = x3.shape
    t = CONVA_T
    vec = _const_spec((1, C_CONV))
    sq = (D_MODEL, D_MODEL)
    return pl.pallas_call(
        _conva_prompt_kernel,
        grid=(nb, seq // t),
        in_specs=[
            pl.BlockSpec((1, t, D_MODEL), lambda b, i: (b, i, 0)),
            vec,
            _const_spec(sq, (0, 0)), _const_spec(sq, (0, 1)), _const_spec(sq, (0, 2)),
            _const_spec((CONF_KERNEL, C_CONV)), vec, vec, vec,
            _const_spec(sq), vec,
        ],
        out_specs=[
            pl.BlockSpec((1, t, D_MODEL), lambda b, i: (b, i, 0)),
            pl.BlockSpec((1, CONF_HALO, C_CONV), lambda b, i: (b, 0, 0)),
        ],
        out_shape=[
            jax.ShapeDtypeStruct((nb, seq, D_MODEL), F32),
            jax.ShapeDtypeStruct((nb, CONF_HALO, C_CONV), F32),
        ],
        scratch_shapes=[
            pltpu.VMEM((CONF_SLABS, t + CONVA_PAD, LANES), F32),
            pltpu.VMEM((t, C_CONV), F32),
            pltpu.VMEM((t, D_MODEL), BF16),
        ],
        compiler_params=_params(2),
        name="conva_prompt",
    )(x3, g_mix, w_main, w_main, w_main, dw_w, dw_b, ln_g, ln_b, w_pw, b_pw)


def _expand_heads(v, g, lo_half):
    rows = v.shape[0]

    def col(e):
        return jnp.broadcast_to(v[:, 4 * g + e:4 * g + e + 1], (rows, LANES))

    return jnp.concatenate([jnp.where(lo_half, col(0), col(1)), jnp.where(lo_half, col(2), col(3))], axis=1)


def _gated_group_norm(y, z, g):
    y = y * _silu(z)
    return y * lax.rsqrt(jnp.mean(y * y, axis=-1, keepdims=True) + EPS) * g


LOG2E = 1.4426950408889634


def _cat3(v):
    return jnp.concatenate(_split3(v), axis=1)


def _ssd_chunk(rows, dt, xs_scr, bc_scr, z_scr, yn_scr, h_ref, a_neg, dexp_ref, ng_ref, ex_ref, acumt_scr):
    t = CHUNK
    r_i = lax.broadcasted_iota(jnp.int32, (t, t), 0)
    c_i = lax.broadcasted_iota(jnp.int32, (t, t), 1)
    causal = r_i >= c_i
    tri = causal.astype(BF16)
    lo_half = lax.broadcasted_iota(jnp.int32, (t, LANES), 1) < HEADDIM
    zeros = jnp.zeros((t, LANES), F32)
    zeros_b = jnp.zeros((t, t), BF16)

    acum = _cumsum_rows(dt * a_neg, tri)
    a_last = acum[t - 1:t, :]
    ealast = jnp.exp(a_last)
    acum2 = acum * LOG2E
    acumt_scr[...] = acum2.T
    dt3 = _cat3(dt)

    def stage_a(g):
        gs = slice(g * GROUP_W, (g + 1) * GROUP_W)
        xs = xs_scr[rows, gs]
        bm = bc_scr[rows, _lanes(g)]
        cm = bc_scr[rows, _lanes(N_GROUPS + g)]
        scores = _dot_nt(cm, bm)
        h_g = h_ref[0, 4 * g:4 * g + 4].reshape(GROUP_W, D_STATE)
        colb = [jnp.broadcast_to(acum2[:, 4 * g + e:4 * g + e + 1], (t, LANES))
                for e in range(HEADS_PER_GROUP)]
        acx = jnp.concatenate([jnp.where(lo_half, colb[0], colb[1]),
                               jnp.where(lo_half, colb[2], colb[3])], axis=1)
        xdt = xs * _dot(dt3, ex_ref[:, gs])
        y = _dot_nt(cm, h_g.astype(BF16)) * jnp.exp2(acx) + xs * dexp_ref[:, gs]
        ms, xblk = [], []
        for e in range(HEADS_PER_GROUP):
            hd = 4 * g + e
            seg = colb[e] - acumt_scr[hd:hd + 1, :]
            m = (scores * jnp.exp2(seg)).astype(BF16)
            ms.append(jnp.where(causal, m, zeros_b))
            half = xdt[:, _lanes(e // 2)]
            keep = jnp.where(lo_half, half, 0.0) if e % 2 == 0 else jnp.where(lo_half, 0.0, half)
            blk = jnp.concatenate([keep, zeros] if e < 2 else [zeros, keep], axis=1)
            xblk.append(blk.astype(BF16))
        xw = (xdt * jnp.exp2(acx[t - 1:t, :] - acx)).astype(BF16)
        return y, jnp.concatenate(ms, axis=1), jnp.concatenate(xblk, axis=0), xw, bm

    def stage_b(g, staged):
        y, mcat, xcat, xw, bm = staged
        gs = slice(g * GROUP_W, (g + 1) * GROUP_W)
        y = y + _dot(mcat, xcat)
        upd = _dot_tn(xw, bm)
        for e in range(HEADS_PER_GROUP):
            hd = 4 * g + e
            h_ref[0, hd] = h_ref[0, hd] * ealast[:, hd:hd + 1] + upd[e * HEADDIM:(e + 1) * HEADDIM, :]
        yn_scr[rows, gs] = _gated_group_norm(y, z_scr[rows, gs], ng_ref[:, gs]).astype(BF16)

    staged = stage_a(0)
    for g in range(N_GROUPS):
        following = stage_a(g + 1) if g + 1 < N_GROUPS else None
        stage_b(g, staged)
        staged = following


SSDP_T = 256
SSDP_RB = 64
SSD_PAD = 8


def _ssd_prompt_kernel(x_ref, gmix_ref, wz_ref, wx_ref, wbc_ref, wdt_ref, bdt_ref, wgb_ref, wssm_ref,
                       cw_ref, cb_ref, alog_ref, dexp_ref, ng_ref, ex_ref,
                       gb_ref, ncv_ref, h_ref,
                       xe, u_scr, xs_scr, bc_scr, z_scr, yn_scr, acumt_scr):
    t = SSDP_T

    @pl.when(pl.program_id(1) == 0)
    def _():
        xe[:, 0:SSD_PAD, :] = jnp.zeros((SSM_SLABS, SSD_PAD, LANES), F32)
        h_ref[...] = jnp.zeros(h_ref.shape, F32)

    u_scr[...] = _rmsnorm(x_ref[0], gmix_ref[...]).astype(BF16)
    a_neg = -jnp.exp(alog_ref[...])
    off = SSD_PAD - SSM_HALO
    for c in range(t // CHUNK):
        c0 = c * CHUNK
        rows = slice(c0, c0 + CHUNK)
        u_c = u_scr[rows, :]
        z_scr[rows, :] = _dot(u_c, wz_ref[...])
        dt = _softplus(_dot(u_c, wdt_ref[...]) + bdt_ref[...])
        xr = _dot(u_c, wx_ref[...])
        for j in range(X_SLABS):
            xe[j, SSD_PAD + c0:SSD_PAD + c0 + CHUNK, :] = xr[:, _lanes(j)]
        bcr = _dot(u_c, wbc_ref[...])
        for j in range(X_SLABS):
            xe[X_SLABS + j, SSD_PAD + c0:SSD_PAD + c0 + CHUNK, :] = bcr[:, _lanes(j)]
        for j in range(SSM_SLABS):
            for rb in range(CHUNK // SSDP_RB):
                r0 = c0 + rb * SSDP_RB
                acc = jnp.broadcast_to(cb_ref[:, _lanes(j)], (SSDP_RB, LANES))
                for k in range(SSM_CONV):
                    acc = acc + cw_ref[k:k + 1, _lanes(j)] * xe[j, r0 + off + k:r0 + off + k + SSDP_RB, :]
                if j < X_SLABS:
                    xs_scr[r0:r0 + SSDP_RB, _lanes(j)] = _silu(acc)
                else:
                    bc_scr[r0:r0 + SSDP_RB, _lanes(j - X_SLABS)] = _silu(acc).astype(BF16)
        _ssd_chunk(rows, dt, xs_scr, bc_scr, z_scr, yn_scr, h_ref, a_neg, dexp_ref, ng_ref, ex_ref, acumt_scr)
        gb_ref[0, rows, :] = _sigmoid(_dot(u_c, wgb_ref[...])) * _dot(yn_scr[rows, :], wssm_ref[...])
    for j in range(SSM_SLABS):
        ncv_ref[0, :, _lanes(j)] = xe[j, t + off:t + SSD_PAD, :]
        xe[j, 0:SSD_PAD, :] = xe[j, t:t + SSD_PAD, :]


def _ssd_prompt(x3, g_mix, w_main, w_dt, b_dt, w_ssm, conv_w, conv_b, a_log, d_exp, norm_g, ex):
    nb, seq, _ = x3.shape
    t = SSDP_T
    wide = (D_MODEL, D_INNER)
    return pl.pallas_call(
        _ssd_prompt_kernel,
        grid=(nb, seq // t),
        in_specs=[
            pl.BlockSpec((1, t, D_MODEL), lambda b, i: (b, i, 0)),
            _const_spec((1, D_MODEL)),
            _const_spec(wide, (0, 2)), _const_spec(wide, (0, 3)), _const_spec(wide, (0, 4)),
            _const_spec((D_MODEL, DT_PAD)), _const_spec((1, DT_PAD)),
            _const_spec((D_MODEL, D_MODEL), (0, 3)),
            _const_spec((D_INNER, D_MODEL)),
            _const_spec((SSM_CONV, CONV_DIM)), _const_spec((1, CONV_DIM)), _const_spec((1, DT_PAD)),
            _const_spec((1, D_INNER)), _const_spec((1, D_INNER)),
            _const_spec((3 * DT_PAD, D_INNER)),
        ],
        out_specs=[
            pl.BlockSpec((1, t, D_MODEL), lambda b, i: (b, i, 0)),
            pl.BlockSpec((1, SSM_HALO, CONV_DIM), lambda b, i: (b, 0, 0)),
            pl.BlockSpec((1, N_HEADS, HEADDIM, D_STATE), lambda b, i: (b, 0, 0, 0)),
        ],
        out_shape=[
            jax.ShapeDtypeStruct((nb, seq, D_MODEL), F32),
            jax.ShapeDtypeStruct((nb, SSM_HALO, CONV_DIM), F32),
            jax.ShapeDtypeStruct((nb, N_HEADS, HEADDIM, D_STATE), F32),
        ],
        scratch_shapes=[
            pltpu.VMEM((SSM_SLABS, t + SSD_PAD, LANES), F32),
            pltpu.VMEM((t, D_MODEL), BF16),
            pltpu.VMEM((t, D_INNER), F32),
            pltpu.VMEM((t, D_INNER), BF16),
            pltpu.VMEM((t, D_INNER), F32),
            pltpu.VMEM((t, D_INNER), BF16),
            pltpu.VMEM((DT_PAD, CHUNK), F32),
        ],
        compiler_params=_params(2),
        name="ssd_prompt",
    )(x3, g_mix, w_main, w_main, w_main, w_dt, b_dt, w_main, w_ssm, conv_w, conv_b, a_log, d_exp, norm_g,
      ex)


def _tail_prompt_kernel(x_ref, ga_ref, gb_ref, wout_ref, gffn_ref, wup_ref, wdown_ref, gfin_ref, y_ref):
    merged = (ga_ref[...] + gb_ref[...]).astype(BF16)
    h = x_ref[...] + _dot(merged, wout_ref[...])
    y_ref[...] = _mlp_final(h, gffn_ref, wup_ref, wdown_ref, gfin_ref)


def _tail_prompt(x2d, ga, gb, w_out, g_ffn, w_up, w_down, g_final, tm=512):
    m = x2d.shape[0]
    tile = pl.BlockSpec((tm, D_MODEL), lambda i: (i, 0))
    vec = _const_spec((1, D_MODEL))
    return pl.pallas_call(
        _tail_prompt_kernel,
        grid=(m // tm,),
        in_specs=[tile, tile, tile, _const_spec((D_MODEL, D_MODEL)), vec,
                  _const_spec((D_MODEL, D_FF)), _const_spec((D_FF, D_MODEL)), vec],
        out_specs=tile,
        out_shape=jax.ShapeDtypeStruct((m, D_MODEL), F32),
        compiler_params=_params(1),
        name="tail_prompt",
    )(x2d, ga, gb, w_out, g_ffn, w_up, w_down, g_final)


def _inproj_kernel(x_ref, g_ref, w_ref, wdt_ref, bdt_ref, proj_ref, dt_ref, u_scr):
    @pl.when(pl.program_id(0) == 0)
    def _():
        ub = _rmsnorm(x_ref[...], g_ref[...]).astype(BF16)
        u_scr[...] = ub
        dt_ref[...] = _softplus(_dot(ub, wdt_ref[...]) + bdt_ref[...])

    proj_ref[...] = _dot(u_scr[...], w_ref[...])


def _inproj(x2d, g_mix, w_main, w_dt, b_dt, tn=1024):
    m = x2d.shape[0]
    return pl.pallas_call(
        _inproj_kernel,
        grid=(MAIN_COLS // tn,),
        in_specs=[
            pl.BlockSpec((m, D_MODEL), lambda j: (0, 0)),
            pl.BlockSpec((1, D_MODEL), lambda j: (0, 0)),
            pl.BlockSpec((D_MODEL, tn), lambda j: (0, j)),
            pl.BlockSpec((D_MODEL, DT_PAD), lambda j: (0, 0)),
            pl.BlockSpec((1, DT_PAD), lambda j: (0, 0)),
        ],
        out_specs=[
            pl.BlockSpec((m, tn), lambda j: (0, j)),
            pl.BlockSpec((m, DT_PAD), lambda j: (0, 0)),
        ],
        out_shape=[
            jax.ShapeDtypeStruct((m, MAIN_COLS), F32),
            jax.ShapeDtypeStruct((m, DT_PAD), F32),
        ],
        scratch_shapes=[pltpu.VMEM((m, D_MODEL), BF16)],
        compiler_params=_params(1),
        name="inproj_sample",
    )(x2d, g_mix, w_main, w_dt, b_dt)


CONVA_SR = 32


def _conva_sample_kernel(a_ref, b_ref, st_ref, w_ref, bias_ref, lng_ref, lnb_ref,
                         ca_ref, nst_ref, g3, c3):
    nreq = st_ref.shape[2]
    steps = a_ref.shape[0] // nreq
    glu = a_ref[...] * _sigmoid(b_ref[...])
    for j in range(CONF_SLABS):
        g3[j] = glu[:, _lanes(j)]

    def plane(h, j):
        if h < CONF_HALO:
            return st_ref[0, h, :, _lanes(j)]
        return g3[j, pl.ds(h - CONF_HALO, nreq, stride=steps), :]

    for j in range(CONF_SLABS):
        for t in range(steps):
            acc = jnp.broadcast_to(bias_ref[:, _lanes(j)], (nreq, LANES))
            for k in range(CONF_KERNEL):
                acc = acc + w_ref[k:k + 1, _lanes(j)] * plane(t + k, j)
            c3[j, pl.ds(t, nreq, stride=steps), :] = acc
        for h in range(CONF_HALO):
            nst_ref[0, h, :, _lanes(j)] = plane(h + steps, j)
    conv = jnp.concatenate([c3[j] for j in range(CONF_SLABS)], axis=1)
    ca_ref[...] = _ln_swish(conv, lng_ref[...], lnb_ref[...]).astype(BF16)


def _conva_sample(proj2d, state_planes, steps, dw_w, dw_b, ln_g, ln_b):
    depth, _, nreq, _ = state_planes.shape
    r = CONVA_SR
    vec = pl.BlockSpec((1, C_CONV), lambda i: (0, 0))
    st_spec = pl.BlockSpec((depth, CONF_HALO, r, C_CONV), lambda i: (0, 0, i, 0))
    slab = pltpu.VMEM((CONF_SLABS, r * steps, LANES), F32)
    return pl.pallas_call(
        _conva_sample_kernel,
        grid=(nreq // r,),
        in_specs=[
            pl.BlockSpec((r * steps, C_CONV), lambda i: (i, 0)),
            pl.BlockSpec((r * steps, C_CONV), lambda i: (i, 1)),
            st_spec,
            pl.BlockSpec((CONF_KERNEL, C_CONV), lambda i: (0, 0)),
            vec, vec, vec,
        ],
        out_specs=[pl.BlockSpec((r * steps, C_CONV), lambda i: (i, 0)), st_spec],
        out_shape=[
            jax.ShapeDtypeStruct((nreq * steps, C_CONV), BF16),
            jax.ShapeDtypeStruct(state_planes.shape, F32),
        ],
        scratch_shapes=[slab, slab],
        compiler_params=_params(1),
        name="conva_sample",
    )(proj2d, proj2d, state_planes, dw_w, dw_b, ln_g, ln_b)


SSD_SR = 8
SSD_ST = 8
SSD_SROWS = 16


def _ssd_sample_kernel(z_ref, xr_ref, bcr_ref, dt_ref, cst_ref, h0_ref, cw_ref, cb_ref, alog_ref,
                       dexp_ref, ng_ref, yn_ref, ncv_ref, h_ref, xe_all, dt8, z8):
    nreq = cst_ref.shape[2]
    steps = z_ref.shape[0] // nreq
    t = SSD_ST
    xe_all[...] = jnp.zeros(xe_all.shape, F32)
    dt8[...] = jnp.zeros(dt8.shape, F32)
    z8[...] = jnp.zeros(z8.shape, F32)
    rows = lax.broadcasted_iota(jnp.int32, (t, 1), 0)
    live = rows < steps
    lo_half = lax.broadcasted_iota(jnp.int32, (t, LANES), 1) < HEADDIM
    a_neg = -jnp.exp(alog_ref[...])
    groups = range(N_GROUPS)

    def request_phases(r, slot):
        rr = slice(r * steps, (r + 1) * steps)
        xs_ = xe_all.at[slot]
        v = {}

        def conv(j):
            acc = jnp.broadcast_to(cb_ref[:, _lanes(j)], (t, LANES))
            for k in range(SSM_CONV):
                acc = acc + cw_ref[k:k + 1, _lanes(j)] * xs_[j, k:k + t, :]
            return jnp.where(live, _silu(acc), 0.0)

        def p_load():
            xr = xr_ref[rr, :]
            bcr = bcr_ref[rr, :]
            for j in range(SSM_SLABS):
                for h in range(SSM_HALO):
                    xs_[j, h:h + 1, :] = cst_ref[0, h, r:r + 1, _lanes(j)]
                new = xr[:, _lanes(j)] if j < X_SLABS else bcr[:, _lanes(j - X_SLABS)]
                xs_[j, SSM_HALO:SSM_HALO + steps, :] = new
                for h in range(SSM_HALO):
                    ncv_ref[0, h, r:r + 1, _lanes(j)] = xs_[j, steps + h:steps + h + 1, :]
            dt8[slot, 0:steps, :] = dt_ref[rr, :]
            z8[slot, 0:steps, :] = z_ref[rr, :]

        def p_conv():
            dt = dt8[slot]
            a = dt * a_neg
            acum = jnp.zeros((t, DT_PAD), F32)
            for s in range(steps):
                acum = acum + jnp.where(rows >= s, a[s:s + 1, :], 0.0)
            tiles = [conv(j) for j in range(SSM_SLABS)]
            v.update(dt=dt, acum=acum, ealast=jnp.exp(acum[t - 1:t, :]),
                     xs=[jnp.concatenate([tiles[2 * g], tiles[2 * g + 1]], axis=1) for g in groups],
                     bm=[tiles[X_SLABS + g] for g in groups],
                     cm=[tiles[X_SLABS + N_GROUPS + g] for g in groups])

        def p_small_products():
            v["scores"] = [_dot_nt(v["cm"][g], v["bm"][g]) for g in groups]
            v["yoff"] = [_dot_nt(v["cm"][g], h0_ref[0, r, 4 * g:4 * g + 4].reshape(GROUP_W, D_STATE))
                         for g in groups]

        def p_spread():
            v["acx"] = [_expand_heads(v["acum"], g, lo_half) for g in groups]
            v["xdt"] = [v["xs"][g] * _expand_heads(v["dt"], g, lo_half) for g in groups]

        def p_vector():
            ys = []
            for g in groups:
                gs = slice(g * GROUP_W, (g + 1) * GROUP_W)
                acx, xdt, scores = v["acx"][g], v["xdt"][g], v["scores"][g]
                y = v["yoff"][g] * jnp.exp(acx) + v["xs"][g] * dexp_ref[:, gs]
                for s in range(steps):
                    decay = jnp.exp(jnp.where(rows >= s, acx - acx[s:s + 1, :], -jnp.inf))
                    y = y + (scores[:, s:s + 1] * decay) * xdt[s:s + 1, :]
                ys.append(y)
            v["ys"] = ys

        def p_state_products():
            v["upd"] = [_dot_tn(v["xdt"][g] * jnp.exp(v["acx"][g][t - 1:t, :] - v["acx"][g]), v["bm"][g])
                        for g in groups]

        def p_state_store():
            for g in groups:
                for e in range(HEADS_PER_GROUP):
                    hd = 4 * g + e
                    h_ref[0, r, hd] = (h0_ref[0, r, hd] * v["ealast"][:, hd:hd + 1]
                                       + v["upd"][g][e * HEADDIM:(e + 1) * HEADDIM, :])

        def p_out():
            for g in groups:
                gs = slice(g * GROUP_W, (g + 1) * GROUP_W)
                yn = _gated_group_norm(v["ys"][g], z8[slot, :, gs], ng_ref[:, gs])
                yn_ref[rr, gs] = yn[0:steps, :]

        return [p_load, p_conv, p_small_products, p_spread, p_vector, p_state_products, p_state_store, p_out]

    for pair in range(nreq // 2):
        for pa, pb in zip(request_phases(2 * pair, 0), request_phases(2 * pair + 1, 1)):
            pa()
            pb()


def _ssd_sample(proj2d, dt2d, conv_state, h0, steps, conv_w, conv_b, a_log, d_exp, norm_g):
    depth, nreq = conv_state.shape[0], conv_state.shape[2]
    r = SSD_SR
    cst_spec = pl.BlockSpec((depth, SSM_HALO, r, CONV_DIM), lambda i: (0, 0, i, 0))
    h_spec = pl.BlockSpec((depth, r, N_HEADS, HEADDIM, D_STATE), lambda i: (0, i, 0, 0, 0))
    return pl.pallas_call(
        _ssd_sample_kernel,
        grid=(nreq // r,),
        in_specs=[
            pl.BlockSpec((r * steps, D_INNER), lambda i: (i, 2)),
            pl.BlockSpec((r * steps, D_INNER), lambda i: (i, 3)),
            pl.BlockSpec((r * steps, D_INNER), lambda i: (i, 4)),
            pl.BlockSpec((r * steps, DT_PAD), lambda i: (i, 0)),
            cst_spec,
            h_spec,
            pl.BlockSpec((SSM_CONV, CONV_DIM), lambda i: (0, 0)),
            pl.BlockSpec((1, CONV_DIM), lambda i: (0, 0)),
            pl.BlockSpec((1, DT_PAD), lambda i: (0, 0)),
            pl.BlockSpec((1, D_INNER), lambda i: (0, 0)),
            pl.BlockSpec((1, D_INNER), lambda i: (0, 0)),
        ],
        out_specs=[pl.BlockSpec((r * steps, D_INNER), lambda i: (i, 0)), cst_spec, h_spec],
        out_shape=[
            jax.ShapeDtypeStruct((nreq * steps, D_INNER), F32),
            jax.ShapeDtypeStruct(conv_state.shape, F32),
            jax.ShapeDtypeStruct(h0.shape, F32),
        ],
        scratch_shapes=[
            pltpu.VMEM((2, SSM_SLABS, SSD_SROWS, LANES), F32),
            pltpu.VMEM((2, SSD_ST, DT_PAD), F32),
            pltpu.VMEM((2, SSD_ST, D_INNER), F32),
        ],
        compiler_params=_params(1),
        name="ssd_sample",
    )(proj2d, proj2d, proj2d, dt2d, conv_state, h0, conv_w, conv_b, a_log, d_exp, norm_g)


def _tail_sample_kernel(x_ref, ca_ref, yn_ref, gate_ref, wpw_ref, bpw_ref, wssm_ref, wout_ref, gffn_ref,
                        wup_ref, wdown_ref, gfin_ref, y_ref):
    branch_a = _dot(ca_ref[...], wpw_ref[...]) + bpw_ref[...]
    branch_b = _dot(yn_ref[...].astype(BF16), wssm_ref[...])
    merged = (_sigmoid(gate_ref[:, 0:D_MODEL]) * branch_a
              + _sigmoid(gate_ref[:, D_MODEL:2 * D_MODEL]) * branch_b)
    h = x_ref[...] + _dot(merged.astype(BF16), wout_ref[...])
    y_ref[...] = _mlp_final(h, gffn_ref, wup_ref, wdown_ref, gfin_ref)


def _tail_sample(x2d, ca, yn, proj2d, w_pw, b_pw, w_ssm, w_out, g_ffn, w_up, w_down, g_final):
    m = x2d.shape[0]
    vec = _const_spec((1, D_MODEL))
    return pl.pallas_call(
        _tail_sample_kernel,
        grid=(1,),
        in_specs=[
            pl.BlockSpec((m, D_MODEL), lambda i: (0, 0)),
            pl.BlockSpec((m, C_CONV), lambda i: (0, 0)),
            pl.BlockSpec((m, D_INNER), lambda i: (0, 0)),
            pl.BlockSpec((m, 2 * D_MODEL), lambda i: (0, 1)),
            _const_spec((C_CONV, D_MODEL)), vec, _const_spec((D_INNER, D_MODEL)),
            _const_spec((D_MODEL, D_MODEL)), vec, _const_spec((D_MODEL, D_FF)),
            _const_spec((D_FF, D_MODEL)), vec,
        ],
        out_specs=pl.BlockSpec((m, D_MODEL), lambda i: (0, 0)),
        out_shape=jax.ShapeDtypeStruct((m, D_MODEL), F32),
        compiler_params=_params(1),
        name="tail_sample",
    )(x2d, ca, yn, proj2d, w_pw, b_pw, w_ssm, w_out, g_ffn, w_up, w_down, g_final)


def kernel(x_prompt, x_sample, state_conf_conv, state_ssm_conv, state_ssm, g_mix, w_in, conf_dw_w,
           conf_dw_b, conf_ln_g, conf_ln_b, conf_w_pw, conf_b_pw, ssm_conv_w, ssm_conv_b, ssm_dt_bias,
           ssm_a_log, ssm_d, ssm_norm_g, ssm_w_out, w_out, g_ffn, w_up, w_down, g_final):
    depth = w_in.shape[0]
    assert depth == 1
    nb, seq, _ = x_prompt.shape
    nreq, steps, _ = x_sample.shape
    i = 0

    def row(v):
        return v.reshape(1, -1)

    def pad_lanes(v, width):
        return jnp.pad(v, ((0, 0), (0, width - v.shape[1])))

    gm = row(g_mix[i])
    w_main = w_in[i].astype(BF16)
    w_dt = pad_lanes(w_main[:, MAIN_COLS:], DT_PAD)
    b_dt = pad_lanes(row(ssm_dt_bias[i]), DT_PAD)
    a_log = pad_lanes(row(ssm_a_log[i]), DT_PAD)
    d_exp = row(jnp.repeat(ssm_d[i], HEADDIM))
    norm_g = row(ssm_norm_g[i])
    dw_w = conf_dw_w[i]
    dw_b, ln_g, ln_b = row(conf_dw_b[i]), row(conf_ln_g[i]), row(conf_ln_b[i])
    cw, cb = ssm_conv_w[i], row(ssm_conv_b[i])
    w_pw, b_pw = conf_w_pw[i].astype(BF16), row(conf_b_pw[i])
    w_ssm = ssm_w_out[i].astype(BF16)
    mlp_w = (w_out[i].astype(BF16), row(g_ffn[i]), w_up[i].astype(BF16), w_down[i].astype(BF16),
             row(g_final))

    ga, conf_p = _conva_prompt(x_prompt, gm, w_main, dw_w, dw_b, ln_g, ln_b, w_pw, b_pw)
    head_row = jnp.arange(3 * DT_PAD, dtype=jnp.int32)[:, None] % DT_PAD
    ex = (head_row == jnp.arange(D_INNER, dtype=jnp.int32)[None, :] // HEADDIM).astype(BF16)
    gb, scv_p, h_p = _ssd_prompt(x_prompt, gm, w_main, w_dt, b_dt, w_ssm, cw, cb, a_log, d_exp, norm_g, ex)
    m_p = nb * seq
    y_p = _tail_prompt(x_prompt.reshape(m_p, D_MODEL), ga.reshape(m_p, D_MODEL), gb.reshape(m_p, D_MODEL),
                       *mlp_w)

    xs = x_sample.reshape(nreq * steps, D_MODEL)
    proj_s, dt_s = _inproj(xs, gm, w_main, w_dt, b_dt)
    planes = (0, 2, 1, 3)
    ca_s, conf_s = _conva_sample(proj_s, state_conf_conv.transpose(planes), steps, dw_w, dw_b, ln_g, ln_b)
    yn_s, scv_s, h_s = _ssd_sample(proj_s, dt_s, state_ssm_conv.transpose(planes), state_ssm, steps, cw, cb,
                                   a_log, d_exp, norm_g)
    y_s = _tail_sample(xs, ca_s, yn_s, proj_s, w_pw, b_pw, w_ssm, *mlp_w)

    return (y_p.reshape(nb, seq, D_MODEL), y_s.reshape(nreq, steps, D_MODEL),
            conf_p[None], scv_p[None], h_p[None], conf_s.transpose(planes), scv_s.transpose(planes), h_s)
```

```python
import jax
import jax.numpy as jnp
from jax import lax
from jax.experimental import pallas as pl
from jax.experimental.pallas import tpu as pltpu

F32 = jnp.float32
BF16 = jnp.bfloat16

D_MODEL = 1024
C_CONV = 1024
CONF_KERNEL = 31
CONF_HALO = CONF_KERNEL - 1
D_INNER = 2048
HEADDIM = 64
N_HEADS = 32
N_GROUPS = 8
HEADS_PER_GROUP = 4
GROUP_W = HEADS_PER_GROUP * HEADDIM
D_STATE = 128
SSM_CONV = 4
SSM_HALO = SSM_CONV - 1
CONV_DIM = 4096
CHUNK = 128
D_FF = 4096
EPS = 1e-6
MAIN_COLS = 10240
DT_PAD = 128
LANES = 128
CONF_SLABS = C_CONV // LANES
SSM_SLABS = CONV_DIM // LANES
X_SLABS = D_INNER // LANES
VMEM_LIMIT = 56 * 1024 * 1024


def _sigmoid(x):
    return jax.nn.sigmoid(x)


def _silu(x):
    return x * jax.nn.sigmoid(x)


def _softplus(x):
    return jnp.maximum(x, 0.0) + jnp.log1p(jnp.exp(-jnp.abs(x)))


def _rmsnorm(x, g):
    return x * lax.rsqrt(jnp.mean(x * x, axis=-1, keepdims=True) + EPS) * g


def _dot(a, b):
    return jnp.dot(a, b, preferred_element_type=F32)


def _dot_nt(a, b):
    return lax.dot_general(a, b, (((1,), (1,)), ((), ())), preferred_element_type=F32)


def _dot_tn(a, b):
    return lax.dot_general(a, b, (((0,), (0,)), ((), ())), preferred_element_type=F32)


def _split3(x):
    hi = x.astype(BF16)
    r1 = x - hi.astype(F32)
    mid = r1.astype(BF16)
    lo = (r1 - mid.astype(F32)).astype(BF16)
    return hi, mid, lo


def _cumsum_rows(a, tri):
    hi, mid, lo = _split3(a)
    return _dot(tri, hi) + _dot(tri, mid) + _dot(tri, lo)


def _lanes(j):
    return slice(j * LANES, (j + 1) * LANES)


def _const_spec(shape, index=None):
    nd = len(shape)
    idx = index if index is not None else (0,) * nd
    return pl.BlockSpec(shape, lambda *_: idx, pipeline_mode=pl.Buffered(1))


def _params(ngrid):
    return pltpu.CompilerParams(dimension_semantics=("arbitrary",) * ngrid, vmem_limit_bytes=VMEM_LIMIT)


def _ln_swish(y, g, b):
    mu = jnp.mean(y, axis=-1, keepdims=True)
    yc = y - mu
    yn = yc * lax.rsqrt(jnp.mean(yc * yc, axis=-1, keepdims=True) + EPS) * g + b
    return _silu(yn)


def _mlp_final(h, gffn_ref, wup_ref, wdown_ref, gfin_ref):
    hidden = jnp.square(jnp.maximum(_dot(_rmsnorm(h, gffn_ref[...]).astype(BF16), wup_ref[...]), 0.0))
    h = h + _dot(hidden.astype(BF16), wdown_ref[...])
    return _rmsnorm(h, gfin_ref[...])


CONVA_T = 512
CONVA_RB = 64
CONVA_PAD = 32


def _conva_prompt_kernel(x_ref, gmix_ref, wa_ref, wb_ref, wga_ref, dw_ref, dwb_ref, lng_ref, lnb_ref,
                         wpw_ref, bpw_ref, ga_ref, st_ref, xe, conv, u_scr):
    t = CONVA_T

    @pl.when(pl.program_id(1) == 0)
    def _():
        xe[:, 0:CONVA_PAD, :] = jnp.zeros((CONF_SLABS, CONVA_PAD, LANES), F32)

    u_scr[...] = _rmsnorm(x_ref[0], gmix_ref[...]).astype(BF16)
    off = CONVA_PAD - CONF_HALO
    glu = _dot(u_scr[...], wa_ref[...]) * _sigmoid(_dot(u_scr[...], wb_ref[...]))
    for j in range(CONF_SLABS):
        xe[j, CONVA_PAD:CONVA_PAD + t, :] = glu[:, _lanes(j)]
    for j in range(CONF_SLABS):
        for rb in range(t // CONVA_RB):
            r0 = rb * CONVA_RB
            acc = jnp.broadcast_to(dwb_ref[:, _lanes(j)], (CONVA_RB, LANES))
            for k in range(CONF_KERNEL):
                acc = acc + dw_ref[k:k + 1, _lanes(j)] * xe[j, r0 + off + k:r0 + off + k + CONVA_RB, :]
            conv[r0:r0 + CONVA_RB, _lanes(j)] = acc
    ca = _ln_swish(conv[...], lng_ref[...], lnb_ref[...]).astype(BF16)
    ga_ref[0] = _sigmoid(_dot(u_scr[...], wga_ref[...])) * (_dot(ca, wpw_ref[...]) + bpw_ref[...])
    for j in range(CONF_SLABS):
        st_ref[0, :, _lanes(j)] = xe[j, t + off:t + CONVA_PAD, :]
        xe[j, 0:CONVA_PAD, :] = xe[j, t:t + CONVA_PAD, :]


def _conva_prompt(x3, g_mix, w_main, dw_w, dw_b, ln_g, ln_b, w_pw, b_pw):
    nb, seq, _ = x3.shape
    t = CONVA_T
    vec = _const_spec((1, C_CONV))
    sq = (D_MODEL, D_MODEL)
    return pl.pallas_call(
        _conva_prompt_kernel,
        grid=(nb, seq // t),
        in_specs=[
            pl.BlockSpec((1, t, D_MODEL), lambda b, i: (b, i, 0)),
            vec,
            _const_spec(sq, (0, 0)), _const_spec(sq, (0, 1)), _const_spec(sq, (0, 2)),
            _const_spec((CONF_KERNEL, C_CONV)), vec, vec, vec,
            _const_spec(sq), vec,
        ],
        out_specs=[
            pl.BlockSpec((1, t, D_MODEL), lambda b, i: (b, i, 0)),
            pl.BlockSpec((1, CONF_HALO, C_CONV), lambda b, i: (b, 0, 0)),
        ],
        out_shape=[
            jax.ShapeDtypeStruct((nb, seq, D_MODEL), F32),
            jax.ShapeDtypeStruct((nb, CONF_HALO, C_CONV), F32),
        ],
        scratch_shapes=[
            pltpu.VMEM((CONF_SLABS, t + CONVA_PAD, LANES), F32),
            pltpu.VMEM((t, C_CONV), F32),
            pltpu.VMEM((t, D_MODEL), BF16),
        ],
        compiler_params=_params(2),
        name="conva_prompt",
    )(x3, g_mix, w_main, w_main, w_main, dw_w, dw_b, ln_g, ln_b, w_pw, b_pw)


def _expand_heads(v, g, lo_half):
    rows = v.shape[0]

    def col(e):
        return jnp.broadcast_to(v[:, 4 * g + e:4 * g + e + 1], (rows, LANES))

    return jnp.concatenate([jnp.where(lo_half, col(0), col(1)), jnp.where(lo_half, col(2), col(3))], axis=1)


def _gated_group_norm(y, z, g):
    y = y * _silu(z)
    return y * lax.rsqrt(jnp.mean(y * y, axis=-1, keepdims=True) + EPS) * g


LOG2E = 1.4426950408889634


def _cat3(v):
    return jnp.concatenate(_split3(v), axis=1)


def _ssd_chunk(rows, dt, conv_slab, z_scr, yn_scr, h_ref, a_neg, dexp_ref, ng_ref, ex_ref, acumt_scr):
    t = CHUNK
    r_i = lax.broadcasted_iota(jnp.int32, (t, t), 0)
    c_i = lax.broadcasted_iota(jnp.int32, (t, t), 1)
    causal = r_i >= c_i
    tri = causal.astype(BF16)
    lo_half = lax.broadcasted_iota(jnp.int32, (t, LANES), 1) < HEADDIM
    zeros = jnp.zeros((t, LANES), F32)
    zeros_b = jnp.zeros((t, t), BF16)

    acum = _cumsum_rows(dt * a_neg, tri)
    a_last = acum[t - 1:t, :]
    ealast = jnp.exp(a_last)
    acum2 = acum * LOG2E
    acumt_scr[...] = acum2.T
    dt3 = _cat3(dt)

    def stage_a(g):
        gs = slice(g * GROUP_W, (g + 1) * GROUP_W)
        xs = jnp.concatenate([conv_slab(2 * g), conv_slab(2 * g + 1)], axis=1)
        bm = conv_slab(X_SLABS + g).astype(BF16)
        cm = conv_slab(X_SLABS + N_GROUPS + g).astype(BF16)
        scores = _dot_nt(cm, bm)
        h_g = h_ref[0, 4 * g:4 * g + 4].reshape(GROUP_W, D_STATE)
        colb = [jnp.broadcast_to(acum2[:, 4 * g + e:4 * g + e + 1], (t, LANES))
                for e in range(HEADS_PER_GROUP)]
        acx = jnp.concatenate([jnp.where(lo_half, colb[0], colb[1]),
                               jnp.where(lo_half, colb[2], colb[3])], axis=1)
        xdt = xs * _dot(dt3, ex_ref[:, gs])
        y = _dot_nt(cm, h_g.astype(BF16)) * jnp.exp2(acx) + xs * dexp_ref[:, gs]
        ms, xblk = [], []
        for e in range(HEADS_PER_GROUP):
            hd = 4 * g + e
            seg = colb[e] - acumt_scr[hd:hd + 1, :]
            m = (scores * jnp.exp2(seg)).astype(BF16)
            ms.append(jnp.where(causal, m, zeros_b))
            half = xdt[:, _lanes(e // 2)]
            keep = jnp.where(lo_half, half, 0.0) if e % 2 == 0 else jnp.where(lo_half, 0.0, half)
            blk = jnp.concatenate([keep, zeros] if e < 2 else [zeros, keep], axis=1)
            xblk.append(blk.astype(BF16))
        xw = (xdt * jnp.exp2(acx[t - 1:t, :] - acx)).astype(BF16)
        return y, jnp.concatenate(ms, axis=1), jnp.concatenate(xblk, axis=0), xw, bm

    def stage_b(g, staged):
        y, mcat, xcat, xw, bm = staged
        gs = slice(g * GROUP_W, (g + 1) * GROUP_W)
        y = y + _dot(mcat, xcat)
        upd = _dot_tn(xw, bm)
        for e in range(HEADS_PER_GROUP):
            hd = 4 * g + e
            h_ref[0, hd] = h_ref[0, hd] * ealast[:, hd:hd + 1] + upd[e * HEADDIM:(e + 1) * HEADDIM, :]
        yn_scr[rows, gs] = _gated_group_norm(y, z_scr[rows, gs], ng_ref[:, gs]).astype(BF16)

    staged = stage_a(0)
    for g in range(N_GROUPS):
        following = stage_a(g + 1) if g + 1 < N_GROUPS else None
        stage_b(g, staged)
        staged = following


SSDP_T = 256
SSD_PAD = 8


def _ssd_prompt_kernel(x_ref, gmix_ref, wz_ref, wx_ref, wbc_ref, wdt_ref, bdt_ref, wgb_ref, wssm_ref,
                       cw_ref, cb_ref, alog_ref, dexp_ref, ng_ref, ex_ref,
                       gb_ref, ncv_ref, h_ref,
                       xe, u_scr, z_scr, yn_scr, acumt_scr):
    t = SSDP_T

    @pl.when(pl.program_id(1) == 0)
    def _():
        xe[:, 0:SSD_PAD, :] = jnp.zeros((SSM_SLABS, SSD_PAD, LANES), F32)
        h_ref[...] = jnp.zeros(h_ref.shape, F32)

    u_scr[...] = _rmsnorm(x_ref[0], gmix_ref[...]).astype(BF16)
    a_neg = -jnp.exp(alog_ref[...])
    off = SSD_PAD - SSM_HALO
    for c in range(t // CHUNK):
        c0 = c * CHUNK
        rows = slice(c0, c0 + CHUNK)
        u_c = u_scr[rows, :]
        z_scr[rows, :] = _dot(u_c, wz_ref[...])
        dt = _softplus(_dot(u_c, wdt_ref[...]) + bdt_ref[...])
        xr = _dot(u_c, wx_ref[...])
        for j in range(X_SLABS):
            xe[j, SSD_PAD + c0:SSD_PAD + c0 + CHUNK, :] = xr[:, _lanes(j)]
        bcr = _dot(u_c, wbc_ref[...])
        for j in range(X_SLABS):
            xe[X_SLABS + j, SSD_PAD + c0:SSD_PAD + c0 + CHUNK, :] = bcr[:, _lanes(j)]
        def conv_slab(j, c0=c0):
            acc = jnp.broadcast_to(cb_ref[:, _lanes(j)], (CHUNK, LANES))
            for k in range(SSM_CONV):
                acc = acc + cw_ref[k:k + 1, _lanes(j)] * xe[j, c0 + off + k:c0 + off + k + CHUNK, :]
            return _silu(acc)

        _ssd_chunk(rows, dt, conv_slab, z_scr, yn_scr, h_ref, a_neg, dexp_ref, ng_ref, ex_ref, acumt_scr)
        gb_ref[0, rows, :] = _sigmoid(_dot(u_c, wgb_ref[...])) * _dot(yn_scr[rows, :], wssm_ref[...])
    for j in range(SSM_SLABS):
        ncv_ref[0, :, _lanes(j)] = xe[j, t + off:t + SSD_PAD, :]
        xe[j, 0:SSD_PAD, :] = xe[j, t:t + SSD_PAD, :]


def _ssd_prompt(x3, g_mix, w_main, w_dt, b_dt, w_ssm, conv_w, conv_b, a_log, d_exp, norm_g, ex):
    nb, seq, _ = x3.shape
    t = SSDP_T
    wide = (D_MODEL, D_INNER)
    return pl.pallas_call(
        _ssd_prompt_kernel,
        grid=(nb, seq // t),
        in_specs=[
            pl.BlockSpec((1, t, D_MODEL), lambda b, i: (b, i, 0)),
            _const_spec((1, D_MODEL)),
            _const_spec(wide, (0, 2)), _const_spec(wide, (0, 3)), _const_spec(wide, (0, 4)),
            _const_spec((D_MODEL, DT_PAD)), _const_spec((1, DT_PAD)),
            _const_spec((D_MODEL, D_MODEL), (0, 3)),
            _const_spec((D_INNER, D_MODEL)),
            _const_spec((SSM_CONV, CONV_DIM)), _const_spec((1, CONV_DIM)), _const_spec((1, DT_PAD)),
            _const_spec((1, D_INNER)), _const_spec((1, D_INNER)),
            _const_spec((3 * DT_PAD, D_INNER)),
        ],
        out_specs=[
            pl.BlockSpec((1, t, D_MODEL), lambda b, i: (b, i, 0)),
            pl.BlockSpec((1, SSM_HALO, CONV_DIM), lambda b, i: (b, 0, 0)),
            pl.BlockSpec((1, N_HEADS, HEADDIM, D_STATE), lambda b, i: (b, 0, 0, 0)),
        ],
        out_shape=[
            jax.ShapeDtypeStruct((nb, seq, D_MODEL), F32),
            jax.ShapeDtypeStruct((nb, SSM_HALO, CONV_DIM), F32),
            jax.ShapeDtypeStruct((nb, N_HEADS, HEADDIM, D_STATE), F32),
        ],
        scratch_shapes=[
            pltpu.VMEM((SSM_SLABS, t + SSD_PAD, LANES), F32),
            pltpu.VMEM((t, D_MODEL), BF16),
            pltpu.VMEM((t, D_INNER), F32),
            pltpu.VMEM((t, D_INNER), BF16),
            pltpu.VMEM((DT_PAD, CHUNK), F32),
        ],
        compiler_params=_params(2),
        name="ssd_prompt",
    )(x3, g_mix, w_main, w_main, w_main, w_dt, b_dt, w_main, w_ssm, conv_w, conv_b, a_log, d_exp, norm_g,
      ex)


def _tail_prompt_kernel(x_ref, ga_ref, gb_ref, wout_ref, gffn_ref, wup_ref, wdown_ref, gfin_ref, y_ref):
    merged = (ga_ref[...] + gb_ref[...]).astype(BF16)
    h = x_ref[...] + _dot(merged, wout_ref[...])
    y_ref[...] = _mlp_final(h, gffn_ref, wup_ref, wdown_ref, gfin_ref)


def _tail_prompt(x2d, ga, gb, w_out, g_ffn, w_up, w_down, g_final, tm=512):
    m = x2d.shape[0]
    tile = pl.BlockSpec((tm, D_MODEL), lambda i: (i, 0))
    vec = _const_spec((1, D_MODEL))
    return pl.pallas_call(
        _tail_prompt_kernel,
        grid=(m // tm,),
        in_specs=[tile, tile, tile, _const_spec((D_MODEL, D_MODEL)), vec,
                  _const_spec((D_MODEL, D_FF)), _const_spec((D_FF, D_MODEL)), vec],
        out_specs=tile,
        out_shape=jax.ShapeDtypeStruct((m, D_MODEL), F32),
        compiler_params=_params(1),
        name="tail_prompt",
    )(x2d, ga, gb, w_out, g_ffn, w_up, w_down, g_final)


def _inproj_kernel(x_ref, g_ref, w_ref, wdt_ref, bdt_ref, proj_ref, dt_ref, u_scr):
    @pl.when(pl.program_id(0) == 0)
    def _():
        ub = _rmsnorm(x_ref[...], g_ref[...]).astype(BF16)
        u_scr[...] = ub
        dt_ref[...] = _softplus(_dot(ub, wdt_ref[...]) + bdt_ref[...])

    proj_ref[...] = _dot(u_scr[...], w_ref[...])


def _inproj(x2d, g_mix, w_main, w_dt, b_dt, tn=1024):
    m = x2d.shape[0]
    return pl.pallas_call(
        _inproj_kernel,
        grid=(MAIN_COLS // tn,),
        in_specs=[
            pl.BlockSpec((m, D_MODEL), lambda j: (0, 0)),
            pl.BlockSpec((1, D_MODEL), lambda j: (0, 0)),
            pl.BlockSpec((D_MODEL, tn), lambda j: (0, j)),
            pl.BlockSpec((D_MODEL, DT_PAD), lambda j: (0, 0)),
            pl.BlockSpec((1, DT_PAD), lambda j: (0, 0)),
        ],
        out_specs=[
            pl.BlockSpec((m, tn), lambda j: (0, j)),
            pl.BlockSpec((m, DT_PAD), lambda j: (0, 0)),
        ],
        out_shape=[
            jax.ShapeDtypeStruct((m, MAIN_COLS), F32),
            jax.ShapeDtypeStruct((m, DT_PAD), F32),
        ],
        scratch_shapes=[pltpu.VMEM((m, D_MODEL), BF16)],
        compiler_params=_params(1),
        name="inproj_sample",
    )(x2d, g_mix, w_main, w_dt, b_dt)


CONVA_SR = 32


def _conva_sample_kernel(a_ref, b_ref, st_ref, w_ref, bias_ref, lng_ref, lnb_ref,
                         ca_ref, nst_ref, g3, c3):
    nreq = st_ref.shape[2]
    steps = a_ref.shape[0] // nreq
    glu = a_ref[...] * _sigmoid(b_ref[...])
    for j in range(CONF_SLABS):
        g3[j] = glu[:, _lanes(j)]

    def plane(h, j):
        if h < CONF_HALO:
            return st_ref[0, h, :, _lanes(j)]
        return g3[j, pl.ds(h - CONF_HALO, nreq, stride=steps), :]

    for j in range(CONF_SLABS):
        for t in range(steps):
            acc = jnp.broadcast_to(bias_ref[:, _lanes(j)], (nreq, LANES))
            for k in range(CONF_KERNEL):
                acc = acc + w_ref[k:k + 1, _lanes(j)] * plane(t + k, j)
            c3[j, pl.ds(t, nreq, stride=steps), :] = acc
        for h in range(CONF_HALO):
            nst_ref[0, h, :, _lanes(j)] = plane(h + steps, j)
    conv = jnp.concatenate([c3[j] for j in range(CONF_SLABS)], axis=1)
    ca_ref[...] = _ln_swish(conv, lng_ref[...], lnb_ref[...]).astype(BF16)


def _conva_sample(proj2d, state_planes, steps, dw_w, dw_b, ln_g, ln_b):
    depth, _, nreq, _ = state_planes.shape
    r = CONVA_SR
    vec = pl.BlockSpec((1, C_CONV), lambda i: (0, 0))
    st_spec = pl.BlockSpec((depth, CONF_HALO, r, C_CONV), lambda i: (0, 0, i, 0))
    slab = pltpu.VMEM((CONF_SLABS, r * steps, LANES), F32)
    return pl.pallas_call(
        _conva_sample_kernel,
        grid=(nreq // r,),
        in_specs=[
            pl.BlockSpec((r * steps, C_CONV), lambda i: (i, 0)),
            pl.BlockSpec((r * steps, C_CONV), lambda i: (i, 1)),
            st_spec,
            pl.BlockSpec((CONF_KERNEL, C_CONV), lambda i: (0, 0)),
            vec, vec, vec,
        ],
        out_specs=[pl.BlockSpec((r * steps, C_CONV), lambda i: (i, 0)), st_spec],
        out_shape=[
            jax.ShapeDtypeStruct((nreq * steps, C_CONV), BF16),
            jax.ShapeDtypeStruct(state_planes.shape, F32),
        ],
        scratch_shapes=[slab, slab],
        compiler_params=_params(1),
        name="conva_sample",
    )(proj2d, proj2d, state_planes, dw_w, dw_b, ln_g, ln_b)


SSD_SR = 8
SSD_ST = 8
SSD_SROWS = 16


def _ssd_sample_kernel(z_ref, xr_ref, bcr_ref, dt_ref, cst_ref, h0_ref, cw_ref, cb_ref, alog_ref,
                       dexp_ref, ng_ref, yn_ref, ncv_ref, h_ref, xe_all, dt8, z8):
    nreq = cst_ref.shape[2]
    steps = z_ref.shape[0] // nreq
    t = SSD_ST
    xe_all[...] = jnp.zeros(xe_all.shape, F32)
    dt8[...] = jnp.zeros(dt8.shape, F32)
    z8[...] = jnp.zeros(z8.shape, F32)
    rows = lax.broadcasted_iota(jnp.int32, (t, 1), 0)
    live = rows < steps
    lo_half = lax.broadcasted_iota(jnp.int32, (t, LANES), 1) < HEADDIM
    a_neg = -jnp.exp(alog_ref[...])
    groups = range(N_GROUPS)

    def request_phases(r, slot):
        rr = slice(r * steps, (r + 1) * steps)
        xs_ = xe_all.at[slot]
        v = {}

        def conv(j):
            acc = jnp.broadcast_to(cb_ref[:, _lanes(j)], (t, LANES))
            for k in range(SSM_CONV):
                acc = acc + cw_ref[k:k + 1, _lanes(j)] * xs_[j, k:k + t, :]
            return jnp.where(live, _silu(acc), 0.0)

        def p_load():
            xr = xr_ref[rr, :]
            bcr = bcr_ref[rr, :]
            for j in range(SSM_SLABS):
                for h in range(SSM_HALO):
                    xs_[j, h:h + 1, :] = cst_ref[0, h, r:r + 1, _lanes(j)]
                new = xr[:, _lanes(j)] if j < X_SLABS else bcr[:, _lanes(j - X_SLABS)]
                xs_[j, SSM_HALO:SSM_HALO + steps, :] = new
                for h in range(SSM_HALO):
                    ncv_ref[0, h, r:r + 1, _lanes(j)] = xs_[j, steps + h:steps + h + 1, :]
            dt8[slot, 0:steps, :] = dt_ref[rr, :]
            z8[slot, 0:steps, :] = z_ref[rr, :]

        def p_conv():
            dt = dt8[slot]
            a = dt * a_neg
            acum = jnp.zeros((t, DT_PAD), F32)
            for s in range(steps):
                acum = acum + jnp.where(rows >= s, a[s:s + 1, :], 0.0)
            tiles = [conv(j) for j in range(SSM_SLABS)]
            v.update(dt=dt, acum=acum, ealast=jnp.exp(acum[t - 1:t, :]),
                     xs=[jnp.concatenate([tiles[2 * g], tiles[2 * g + 1]], axis=1) for g in groups],
                     bm=[tiles[X_SLABS + g] for g in groups],
                     cm=[tiles[X_SLABS + N_GROUPS + g] for g in groups])

        def p_small_products():
            v["scores"] = [_dot_nt(v["cm"][g], v["bm"][g]) for g in groups]
            v["yoff"] = [_dot_nt(v["cm"][g], h0_ref[0, r, 4 * g:4 * g + 4].reshape(GROUP_W, D_STATE))
                         for g in groups]

        def p_spread():
            v["acx"] = [_expand_heads(v["acum"], g, lo_half) for g in groups]
            v["xdt"] = [v["xs"][g] * _expand_heads(v["dt"], g, lo_half) for g in groups]

        def p_vector():
            ys = []
            for g in groups:
                gs = slice(g * GROUP_W, (g + 1) * GROUP_W)
                acx, xdt, scores = v["acx"][g], v["xdt"][g], v["scores"][g]
                y = v["yoff"][g] * jnp.exp(acx) + v["xs"][g] * dexp_ref[:, gs]
                for s in range(steps):
                    decay = jnp.exp(jnp.where(rows >= s, acx - acx[s:s + 1, :], -jnp.inf))
                    y = y + (scores[:, s:s + 1] * decay) * xdt[s:s + 1, :]
                ys.append(y)
            v["ys"] = ys

        def p_state_products():
            v["upd"] = [_dot_tn(v["xdt"][g] * jnp.exp(v["acx"][g][t - 1:t, :] - v["acx"][g]), v["bm"][g])
                        for g in groups]

        def p_state_store():
            for g in groups:
                for e in range(HEADS_PER_GROUP):
                    hd = 4 * g + e
                    h_ref[0, r, hd] = (h0_ref[0, r, hd] * v["ealast"][:, hd:hd + 1]
                                       + v["upd"][g][e * HEADDIM:(e + 1) * HEADDIM, :])

        def p_out():
            for g in groups:
                gs = slice(g * GROUP_W, (g + 1) * GROUP_W)
                yn = _gated_group_norm(v["ys"][g], z8[slot, :, gs], ng_ref[:, gs])
                yn_ref[rr, gs] = yn[0:steps, :]

        return [p_load, p_conv, p_small_products, p_spread, p_vector, p_state_products, p_state_store, p_out]

    for pair in range(nreq // 2):
        for pa, pb in zip(request_phases(2 * pair, 0), request_phases(2 * pair + 1, 1)):
            pa()
            pb()


def _ssd_sample(proj2d, dt2d, conv_state, h0, steps, conv_w, conv_b, a_log, d_exp, norm_g):
    depth, nreq = conv_state.shape[0], conv_state.shape[2]
    r = SSD_SR
    cst_spec = pl.BlockSpec((depth, SSM_HALO, r, CONV_DIM), lambda i: (0, 0, i, 0))
    h_spec = pl.BlockSpec((depth, r, N_HEADS, HEADDIM, D_STATE), lambda i: (0, i, 0, 0, 0))
    return pl.pallas_call(
        _ssd_sample_kernel,
        grid=(nreq // r,),
        in_specs=[
            pl.BlockSpec((r * steps, D_INNER), lambda i: (i, 2)),
            pl.BlockSpec((r * steps, D_INNER), lambda i: (i, 3)),
            pl.BlockSpec((r * steps, D_INNER), lambda i: (i, 4)),
            pl.BlockSpec((r * steps, DT_PAD), lambda i: (i, 0)),
            cst_spec,
            h_spec,
            pl.BlockSpec((SSM_CONV, CONV_DIM), lambda i: (0, 0)),
            pl.BlockSpec((1, CONV_DIM), lambda i: (0, 0)),
            pl.BlockSpec((1, DT_PAD), lambda i: (0, 0)),
            pl.BlockSpec((1, D_INNER), lambda i: (0, 0)),
            pl.BlockSpec((1, D_INNER), lambda i: (0, 0)),
        ],
        out_specs=[pl.BlockSpec((r * steps, D_INNER), lambda i: (i, 0)), cst_spec, h_spec],
        out_shape=[
            jax.ShapeDtypeStruct((nreq * steps, D_INNER), F32),
            jax.ShapeDtypeStruct(conv_state.shape, F32),
            jax.ShapeDtypeStruct(h0.shape, F32),
        ],
        scratch_shapes=[
            pltpu.VMEM((2, SSM_SLABS, SSD_SROWS, LANES), F32),
            pltpu.VMEM((2, SSD_ST, DT_PAD), F32),
            pltpu.VMEM((2, SSD_ST, D_INNER), F32),
        ],
        compiler_params=_params(1),
        name="ssd_sample",
    )(proj2d, proj2d, proj2d, dt2d, conv_state, h0, conv_w, conv_b, a_log, d_exp, norm_g)


def _tail_sample_kernel(x_ref, ca_ref, yn_ref, gate_ref, wpw_ref, bpw_ref, wssm_ref, wout_ref, gffn_ref,
                        wup_ref, wdown_ref, gfin_ref, y_ref):
    branch_a = _dot(ca_ref[...], wpw_ref[...]) + bpw_ref[...]
    branch_b = _dot(yn_ref[...].astype(BF16), wssm_ref[...])
    merged = (_sigmoid(gate_ref[:, 0:D_MODEL]) * branch_a
              + _sigmoid(gate_ref[:, D_MODEL:2 * D_MODEL]) * branch_b)
    h = x_ref[...] + _dot(merged.astype(BF16), wout_ref[...])
    y_ref[...] = _mlp_final(h, gffn_ref, wup_ref, wdown_ref, gfin_ref)


def _tail_sample(x2d, ca, yn, proj2d, w_pw, b_pw, w_ssm, w_out, g_ffn, w_up, w_down, g_final):
    m = x2d.shape[0]
    vec = _const_spec((1, D_MODEL))
    return pl.pallas_call(
        _tail_sample_kernel,
        grid=(1,),
        in_specs=[
            pl.BlockSpec((m, D_MODEL), lambda i: (0, 0)),
            pl.BlockSpec((m, C_CONV), lambda i: (0, 0)),
            pl.BlockSpec((m, D_INNER), lambda i: (0, 0)),
            pl.BlockSpec((m, 2 * D_MODEL), lambda i: (0, 1)),
            _const_spec((C_CONV, D_MODEL)), vec, _const_spec((D_INNER, D_MODEL)),
            _const_spec((D_MODEL, D_MODEL)), vec, _const_spec((D_MODEL, D_FF)),
            _const_spec((D_FF, D_MODEL)), vec,
        ],
        out_specs=pl.BlockSpec((m, D_MODEL), lambda i: (0, 0)),
        out_shape=jax.ShapeDtypeStruct((m, D_MODEL), F32),
        compiler_params=_params(1),
        name="tail_sample",
    )(x2d, ca, yn, proj2d, w_pw, b_pw, w_ssm, w_out, g_ffn, w_up, w_down, g_final)


def kernel(x_prompt, x_sample, state_conf_conv, state_ssm_conv, state_ssm, g_mix, w_in, conf_dw_w,
           conf_dw_b, conf_ln_g, conf_ln_b, conf_w_pw, conf_b_pw, ssm_conv_w, ssm_conv_b, ssm_dt_bias,
           ssm_a_log, ssm_d, ssm_norm_g, ssm_w_out, w_out, g_ffn, w_up, w_down, g_final):
    depth = w_in.shape[0]
    assert depth == 1
    nb, seq, _ = x_prompt.shape
    nreq, steps, _ = x_sample.shape
    i = 0

    def row(v):
        return v.reshape(1, -1)

    def pad_lanes(v, width):
        return jnp.pad(v, ((0, 0), (0, width - v.shape[1])))

    gm = row(g_mix[i])
    w_main = w_in[i].astype(BF16)
    w_dt = pad_lanes(w_main[:, MAIN_COLS:], DT_PAD)
    b_dt = pad_lanes(row(ssm_dt_bias[i]), DT_PAD)
    a_log = pad_lanes(row(ssm_a_log[i]), DT_PAD)
    d_exp = row(jnp.repeat(ssm_d[i], HEADDIM))
    norm_g = row(ssm_norm_g[i])
    dw_w = conf_dw_w[i]
    dw_b, ln_g, ln_b = row(conf_dw_b[i]), row(conf_ln_g[i]), row(conf_ln_b[i])
    cw, cb = ssm_conv_w[i], row(ssm_conv_b[i])
    w_pw, b_pw = conf_w_pw[i].astype(BF16), row(conf_b_pw[i])
    w_ssm = ssm_w_out[i].astype(BF16)
    mlp_w = (w_out[i].astype(BF16), row(g_ffn[i]), w_up[i].astype(BF16), w_down[i].astype(BF16),
             row(g_final))

    ga, conf_p = _conva_prompt(x_prompt, gm, w_main, dw_w, dw_b, ln_g, ln_b, w_pw, b_pw)
    head_row = jnp.arange(3 * DT_PAD, dtype=jnp.int32)[:, None] % DT_PAD
    ex = (head_row == jnp.arange(D_INNER, dtype=jnp.int32)[None, :] // HEADDIM).astype(BF16)
    gb, scv_p, h_p = _ssd_prompt(x_prompt, gm, w_main, w_dt, b_dt, w_ssm, cw, cb, a_log, d_exp, norm_g, ex)
    m_p = nb * seq
    y_p = _tail_prompt(x_prompt.reshape(m_p, D_MODEL), ga.reshape(m_p, D_MODEL), gb.reshape(m_p, D_MODEL),
                       *mlp_w)

    xs = x_sample.reshape(nreq * steps, D_MODEL)
    proj_s, dt_s = _inproj(xs, gm, w_main, w_dt, b_dt)
    planes = (0, 2, 1, 3)
    ca_s, conf_s = _conva_sample(proj_s, state_conf_conv.transpose(planes), steps, dw_w, dw_b, ln_g, ln_b)
    yn_s, scv_s, h_s = _ssd_sample(proj_s, dt_s, state_ssm_conv.transpose(planes), state_ssm, steps, cw, cb,
                                   a_log, d_exp, norm_g)
    y_s = _tail_sample(xs, ca_s, yn_s, proj_s, w_pw, b_pw, w_ssm, *mlp_w)

    return (y_p.reshape(nb, seq, D_MODEL), y_s.reshape(nreq, steps, D_MODEL),
            conf_p[None], scv_p[None], h_p[None], conf_s.transpose(planes), scv_s.transpose(planes), h_s)
```

```python
import jax
import jax.numpy as jnp
from jax import lax
from jax.experimental import pallas as pl
from jax.experimental.pallas import tpu as pltpu

F32 = jnp.float32
BF16 = jnp.bfloat16

D_MODEL = 1024
C_CONV = 1024
CONF_KERNEL = 31
CONF_HALO = CONF_KERNEL - 1
D_INNER = 2048
HEADDIM = 64
N_HEADS = 32
N_GROUPS = 8
HEADS_PER_GROUP = 4
GROUP_W = HEADS_PER_GROUP * HEADDIM
D_STATE = 128
SSM_CONV = 4
SSM_HALO = SSM_CONV - 1
CONV_DIM = 4096
CHUNK = 128
D_FF = 4096
EPS = 1e-6
MAIN_COLS = 10240
DT_PAD = 128
LANES = 128
CONF_SLABS = C_CONV // LANES
SSM_SLABS = CONV_DIM // LANES
X_SLABS = D_INNER // LANES
VMEM_LIMIT = 56 * 1024 * 1024


def _sigmoid(x):
    return jax.nn.sigmoid(x)


def _silu(x):
    return x * jax.nn.sigmoid(x)


def _softplus(x):
    return jnp.maximum(x, 0.0) + jnp.log1p(jnp.exp(-jnp.abs(x)))


def _rmsnorm(x, g):
    return x * lax.rsqrt(jnp.mean(x * x, axis=-1, keepdims=True) + EPS) * g


def _dot(a, b):
    return jnp.dot(a, b, preferred_element_type=F32)


def _dot_nt(a, b):
    return lax.dot_general(a, b, (((1,), (1,)), ((), ())), preferred_element_type=F32)


def _dot_tn(a, b):
    return lax.dot_general(a, b, (((0,), (0,)), ((), ())), preferred_element_type=F32)


def _split3(x):
    hi = x.astype(BF16)
    r1 = x - hi.astype(F32)
    mid = r1.astype(BF16)
    lo = (r1 - mid.astype(F32)).astype(BF16)
    return hi, mid, lo


def _cumsum_rows(a, tri):
    hi, mid, lo = _split3(a)
    return _dot(tri, hi) + _dot(tri, mid) + _dot(tri, lo)


def _lanes(j):
    return slice(j * LANES, (j + 1) * LANES)


def _const_spec(shape, index=None):
    nd = len(shape)
    idx = index if index is not None else (0,) * nd
    return pl.BlockSpec(shape, lambda *_: idx, pipeline_mode=pl.Buffered(1))


def _params(ngrid):
    return pltpu.CompilerParams(dimension_semantics=("arbitrary",) * ngrid, vmem_limit_bytes=VMEM_LIMIT)


def _ln_swish(y, g, b):
    mu = jnp.mean(y, axis=-1, keepdims=True)
    yc = y - mu
    yn = yc * lax.rsqrt(jnp.mean(yc * yc, axis=-1, keepdims=True) + EPS) * g + b
    return _silu(yn)


def _mlp_final(h, gffn_ref, wup_ref, wdown_ref, gfin_ref):
    hidden = jnp.square(jnp.maximum(_dot(_rmsnorm(h, gffn_ref[...]).astype(BF16), wup_ref[...]), 0.0))
    h = h + _dot(hidden.astype(BF16), wdown_ref[...])
    return _rmsnorm(h, gfin_ref[...])


CONVA_T = 512
CONVA_RB = 32
CONVA_PAD = 32


def _conva_prompt_kernel(x_ref, gmix_ref, wa_ref, wb_ref, wga_ref, dw_ref, dwb_ref, lng_ref, lnb_ref,
                         wpw_ref, bpw_ref, ga_ref, st_ref, xe, conv, u_scr):
    t = CONVA_T

    @pl.when(pl.program_id(1) == 0)
    def _():
        xe[:, 0:CONVA_PAD, :] = jnp.zeros((CONF_SLABS, CONVA_PAD, LANES), F32)

    u_scr[...] = _rmsnorm(x_ref[0], gmix_ref[...]).astype(BF16)
    off = CONVA_PAD - CONF_HALO
    glu = _dot(u_scr[...], wa_ref[...]) * _sigmoid(_dot(u_scr[...], wb_ref[...]))
    for j in range(CONF_SLABS):
        xe[j, CONVA_PAD:CONVA_PAD + t, :] = glu[:, _lanes(j)]
    for rb in range(t // CONVA_RB):
        r0 = rb * CONVA_RB
        blocks = []
        for j in range(CONF_SLABS):
            acc = jnp.broadcast_to(dwb_ref[:, _lanes(j)], (CONVA_RB, LANES))
            for k in range(CONF_KERNEL):
                acc = acc + dw_ref[k:k + 1, _lanes(j)] * xe[j, r0 + off + k:r0 + off + k + CONVA_RB, :]
            blocks.append(acc)
        conv[r0:r0 + CONVA_RB, :] = _ln_swish(jnp.concatenate(blocks, axis=1), lng_ref[...],
                                              lnb_ref[...]).astype(BF16)
    ga_ref[0] = _sigmoid(_dot(u_scr[...], wga_ref[...])) * (_dot(conv[...], wpw_ref[...]) + bpw_ref[...])
    for j in range(CONF_SLABS):
        st_ref[0, :, _lanes(j)] = xe[j, t + off:t + CONVA_PAD, :]
        xe[j, 0:CONVA_PAD, :] = xe[j, t:t + CONVA_PAD, :]


def _conva_prompt(x3, g_mix, w_main, dw_w, dw_b, ln_g, ln_b, w_pw, b_pw):
    nb, seq, _ = x3.shape
    t = CONVA_T
    vec = _const_spec((1, C_CONV))
    sq = (D_MODEL, D_MODEL)
    return pl.pallas_call(
        _conva_prompt_kernel,
        grid=(nb, seq // t),
        in_specs=[
            pl.BlockSpec((1, t, D_MODEL), lambda b, i: (b, i, 0)),
            vec,
            _const_spec(sq, (0, 0)), _const_spec(sq, (0, 1)), _const_spec(sq, (0, 2)),
            _const_spec((CONF_KERNEL, C_CONV)), vec, vec, vec,
            _const_spec(sq), vec,
        ],
        out_specs=[
            pl.BlockSpec((1, t, D_MODEL), lambda b, i: (b, i, 0)),
            pl.BlockSpec((1, CONF_HALO, C_CONV), lambda b, i: (b, 0, 0)),
        ],
        out_shape=[
            jax.ShapeDtypeStruct((nb, seq, D_MODEL), F32),
            jax.ShapeDtypeStruct((nb, CONF_HALO, C_CONV), F32),
        ],
        scratch_shapes=[
            pltpu.VMEM((CONF_SLABS, t + CONVA_PAD, LANES), F32),
            pltpu.VMEM((t, C_CONV), BF16),
            pltpu.VMEM((t, D_MODEL), BF16),
        ],
        compiler_params=_params(2),
        name="conva_prompt",
    )(x3, g_mix, w_main, w_main, w_main, dw_w, dw_b, ln_g, ln_b, w_pw, b_pw)


def _expand_heads(v, g, lo_half):
    rows = v.shape[0]

    def col(e):
        return jnp.broadcast_to(v[:, 4 * g + e:4 * g + e + 1], (rows, LANES))

    return jnp.concatenate([jnp.where(lo_half, col(0), col(1)), jnp.where(lo_half, col(2), col(3))], axis=1)


def _gated_group_norm(y, z, g):
    y = y * _silu(z)
    return y * lax.rsqrt(jnp.mean(y * y, axis=-1, keepdims=True) + EPS) * g


LOG2E = 1.4426950408889634


def _cat3(v):
    return jnp.concatenate(_split3(v), axis=1)


def _ssd_chunk(rows, dt, conv_slab, z_scr, yn_scr, h_ref, a_neg, dexp_ref, ng_ref, ex_ref, acumt_scr):
    t = CHUNK
    r_i = lax.broadcasted_iota(jnp.int32, (t, t), 0)
    c_i = lax.broadcasted_iota(jnp.int32, (t, t), 1)
    causal = r_i >= c_i
    tri = causal.astype(BF16)
    lo_half = lax.broadcasted_iota(jnp.int32, (t, LANES), 1) < HEADDIM
    zeros = jnp.zeros((t, LANES), F32)
    zeros_b = jnp.zeros((t, t), BF16)

    acum = _cumsum_rows(dt * a_neg, tri)
    a_last = acum[t - 1:t, :]
    ealast = jnp.exp(a_last)
    acum2 = acum * LOG2E
    acumt_scr[...] = acum2.T
    dt3 = _cat3(dt)

    def stage_a(g):
        gs = slice(g * GROUP_W, (g + 1) * GROUP_W)
        xs = jnp.concatenate([conv_slab(2 * g), conv_slab(2 * g + 1)], axis=1)
        bm = conv_slab(X_SLABS + g).astype(BF16)
        cm = conv_slab(X_SLABS + N_GROUPS + g).astype(BF16)
        scores = _dot_nt(cm, bm)
        h_g = h_ref[0, 4 * g:4 * g + 4].reshape(GROUP_W, D_STATE)
        colb = [jnp.broadcast_to(acum2[:, 4 * g + e:4 * g + e + 1], (t, LANES))
                for e in range(HEADS_PER_GROUP)]
        acx = jnp.concatenate([jnp.where(lo_half, colb[0], colb[1]),
                               jnp.where(lo_half, colb[2], colb[3])], axis=1)
        xdt = xs * _dot(dt3, ex_ref[:, gs])
        y = _dot_nt(cm, h_g.astype(BF16)) * jnp.exp2(acx) + xs * dexp_ref[:, gs]
        ms, xblk = [], []
        for e in range(HEADS_PER_GROUP):
            hd = 4 * g + e
            seg = colb[e] - acumt_scr[hd:hd + 1, :]
            m = (scores * jnp.exp2(seg)).astype(BF16)
            ms.append(jnp.where(causal, m, zeros_b))
            half = xdt[:, _lanes(e // 2)]
            keep = jnp.where(lo_half, half, 0.0) if e % 2 == 0 else jnp.where(lo_half, 0.0, half)
            blk = jnp.concatenate([keep, zeros] if e < 2 else [zeros, keep], axis=1)
            xblk.append(blk.astype(BF16))
        xw = (xdt * jnp.exp2(acx[t - 1:t, :] - acx)).astype(BF16)
        return y, jnp.concatenate(ms, axis=1), jnp.concatenate(xblk, axis=0), xw, bm

    def stage_b(g, staged):
        y, mcat, xcat, xw, bm = staged
        gs = slice(g * GROUP_W, (g + 1) * GROUP_W)
        y = y + _dot(mcat, xcat)
        upd = _dot_tn(xw, bm)
        for e in range(HEADS_PER_GROUP):
            hd = 4 * g + e
            h_ref[0, hd] = h_ref[0, hd] * ealast[:, hd:hd + 1] + upd[e * HEADDIM:(e + 1) * HEADDIM, :]
        yn_scr[rows, gs] = _gated_group_norm(y, z_scr[rows, gs], ng_ref[:, gs]).astype(BF16)

    staged = stage_a(0)
    for g in range(N_GROUPS):
        following = stage_a(g + 1) if g + 1 < N_GROUPS else None
        stage_b(g, staged)
        staged = following


SSDP_T = 256
SSD_PAD = 8


def _ssd_prompt_kernel(x_ref, gmix_ref, wz_ref, wx_ref, wbc_ref, wdt_ref, bdt_ref, wgb_ref, wssm_ref,
                       cw_ref, cb_ref, alog_ref, dexp_ref, ng_ref, ex_ref,
                       gb_ref, ncv_ref, h_ref,
                       xe, u_scr, z_scr, yn_scr, acumt_scr):
    t = SSDP_T

    @pl.when(pl.program_id(1) == 0)
    def _():
        xe[:, 0:SSD_PAD, :] = jnp.zeros((SSM_SLABS, SSD_PAD, LANES), F32)
        h_ref[...] = jnp.zeros(h_ref.shape, F32)

    u_scr[...] = _rmsnorm(x_ref[0], gmix_ref[...]).astype(BF16)
    a_neg = -jnp.exp(alog_ref[...])
    off = SSD_PAD - SSM_HALO
    for c in range(t // CHUNK):
        c0 = c * CHUNK
        rows = slice(c0, c0 + CHUNK)
        u_c = u_scr[rows, :]
        z_scr[rows, :] = _dot(u_c, wz_ref[...])
        dt = _softplus(_dot(u_c, wdt_ref[...]) + bdt_ref[...])
        xr = _dot(u_c, wx_ref[...])
        for j in range(X_SLABS):
            xe[j, SSD_PAD + c0:SSD_PAD + c0 + CHUNK, :] = xr[:, _lanes(j)]
        bcr = _dot(u_c, wbc_ref[...])
        for j in range(X_SLABS):
            xe[X_SLABS + j, SSD_PAD + c0:SSD_PAD + c0 + CHUNK, :] = bcr[:, _lanes(j)]
        def conv_slab(j, c0=c0):
            acc = jnp.broadcast_to(cb_ref[:, _lanes(j)], (CHUNK, LANES))
            for k in range(SSM_CONV):
                acc = acc + cw_ref[k:k + 1, _lanes(j)] * xe[j, c0 + off + k:c0 + off + k + CHUNK, :]
            return _silu(acc)

        _ssd_chunk(rows, dt, conv_slab, z_scr, yn_scr, h_ref, a_neg, dexp_ref, ng_ref, ex_ref, acumt_scr)
        gb_ref[0, rows, :] = _sigmoid(_dot(u_c, wgb_ref[...])) * _dot(yn_scr[rows, :], wssm_ref[...])
    for j in range(SSM_SLABS):
        ncv_ref[0, :, _lanes(j)] = xe[j, t + off:t + SSD_PAD, :]
        xe[j, 0:SSD_PAD, :] = xe[j, t:t + SSD_PAD, :]


def _ssd_prompt(x3, g_mix, w_main, w_dt, b_dt, w_ssm, conv_w, conv_b, a_log, d_exp, norm_g, ex):
    nb, seq, _ = x3.shape
    t = SSDP_T
    wide = (D_MODEL, D_INNER)
    return pl.pallas_call(
        _ssd_prompt_kernel,
        grid=(nb, seq // t),
        in_specs=[
            pl.BlockSpec((1, t, D_MODEL), lambda b, i: (b, i, 0)),
            _const_spec((1, D_MODEL)),
            _const_spec(wide, (0, 2)), _const_spec(wide, (0, 3)), _const_spec(wide, (0, 4)),
            _const_spec((D_MODEL, DT_PAD)), _const_spec((1, DT_PAD)),
            _const_spec((D_MODEL, D_MODEL), (0, 3)),
            _const_spec((D_INNER, D_MODEL)),
            _const_spec((SSM_CONV, CONV_DIM)), _const_spec((1, CONV_DIM)), _const_spec((1, DT_PAD)),
            _const_spec((1, D_INNER)), _const_spec((1, D_INNER)),
            _const_spec((3 * DT_PAD, D_INNER)),
        ],
        out_specs=[
            pl.BlockSpec((1, t, D_MODEL), lambda b, i: (b, i, 0)),
            pl.BlockSpec((1, SSM_HALO, CONV_DIM), lambda b, i: (b, 0, 0)),
            pl.BlockSpec((1, N_HEADS, HEADDIM, D_STATE), lambda b, i: (b, 0, 0, 0)),
        ],
        out_shape=[
            jax.ShapeDtypeStruct((nb, seq, D_MODEL), F32),
            jax.ShapeDtypeStruct((nb, SSM_HALO, CONV_DIM), F32),
            jax.ShapeDtypeStruct((nb, N_HEADS, HEADDIM, D_STATE), F32),
        ],
        scratch_shapes=[
            pltpu.VMEM((SSM_SLABS, t + SSD_PAD, LANES), F32),
            pltpu.VMEM((t, D_MODEL), BF16),
            pltpu.VMEM((t, D_INNER), F32),
            pltpu.VMEM((t, D_INNER), BF16),
            pltpu.VMEM((DT_PAD, CHUNK), F32),
        ],
        compiler_params=_params(2),
        name="ssd_prompt",
    )(x3, g_mix, w_main, w_main, w_main, w_dt, b_dt, w_main, w_ssm, conv_w, conv_b, a_log, d_exp, norm_g,
      ex)


def _tail_prompt_kernel(x_ref, ga_ref, gb_ref, wout_ref, gffn_ref, wup_ref, wdown_ref, gfin_ref, y_ref):
    merged = (ga_ref[...] + gb_ref[...]).astype(BF16)
    h = x_ref[...] + _dot(merged, wout_ref[...])
    y_ref[...] = _mlp_final(h, gffn_ref, wup_ref, wdown_ref, gfin_ref)


def _tail_prompt(x2d, ga, gb, w_out, g_ffn, w_up, w_down, g_final, tm=512):
    m = x2d.shape[0]
    tile = pl.BlockSpec((tm, D_MODEL), lambda i: (i, 0))
    vec = _const_spec((1, D_MODEL))
    return pl.pallas_call(
        _tail_prompt_kernel,
        grid=(m // tm,),
        in_specs=[tile, tile, tile, _const_spec((D_MODEL, D_MODEL)), vec,
                  _const_spec((D_MODEL, D_FF)), _const_spec((D_FF, D_MODEL)), vec],
        out_specs=tile,
        out_shape=jax.ShapeDtypeStruct((m, D_MODEL), F32),
        compiler_params=_params(1),
        name="tail_prompt",
    )(x2d, ga, gb, w_out, g_ffn, w_up, w_down, g_final)


def _inproj_kernel(x_ref, g_ref, w_ref, wdt_ref, bdt_ref, proj_ref, dt_ref, u_scr):
    @pl.when(pl.program_id(0) == 0)
    def _():
        ub = _rmsnorm(x_ref[...], g_ref[...]).astype(BF16)
        u_scr[...] = ub
        dt_ref[...] = _softplus(_dot(ub, wdt_ref[...]) + bdt_ref[...])

    proj_ref[...] = _dot(u_scr[...], w_ref[...])


def _inproj(x2d, g_mix, w_main, w_dt, b_dt, tn=1024):
    m = x2d.shape[0]
    return pl.pallas_call(
        _inproj_kernel,
        grid=(MAIN_COLS // tn,),
        in_specs=[
            pl.BlockSpec((m, D_MODEL), lambda j: (0, 0)),
            pl.BlockSpec((1, D_MODEL), lambda j: (0, 0)),
            pl.BlockSpec((D_MODEL, tn), lambda j: (0, j)),
            pl.BlockSpec((D_MODEL, DT_PAD), lambda j: (0, 0)),
            pl.BlockSpec((1, DT_PAD), lambda j: (0, 0)),
        ],
        out_specs=[
            pl.BlockSpec((m, tn), lambda j: (0, j)),
            pl.BlockSpec((m, DT_PAD), lambda j: (0, 0)),
        ],
        out_shape=[
            jax.ShapeDtypeStruct((m, MAIN_COLS), F32),
            jax.ShapeDtypeStruct((m, DT_PAD), F32),
        ],
        scratch_shapes=[pltpu.VMEM((m, D_MODEL), BF16)],
        compiler_params=_params(1),
        name="inproj_sample",
    )(x2d, g_mix, w_main, w_dt, b_dt)


CONVA_SR = 32


def _conva_sample_kernel(a_ref, b_ref, st_ref, w_ref, bias_ref, lng_ref, lnb_ref,
                         ca_ref, nst_ref, g3, c3):
    nreq = st_ref.shape[2]
    steps = a_ref.shape[0] // nreq
    glu = a_ref[...] * _sigmoid(b_ref[...])
    for j in range(CONF_SLABS):
        g3[j] = glu[:, _lanes(j)]

    def plane(h, j):
        if h < CONF_HALO:
            return st_ref[0, h, :, _lanes(j)]
        return g3[j, pl.ds(h - CONF_HALO, nreq, stride=steps), :]

    for j in range(CONF_SLABS):
        for t in range(steps):
            acc = jnp.broadcast_to(bias_ref[:, _lanes(j)], (nreq, LANES))
            for k in range(CONF_KERNEL):
                acc = acc + w_ref[k:k + 1, _lanes(j)] * plane(t + k, j)
            c3[j, pl.ds(t, nreq, stride=steps), :] = acc
        for h in range(CONF_HALO):
            nst_ref[0, h, :, _lanes(j)] = plane(h + steps, j)
    conv = jnp.concatenate([c3[j] for j in range(CONF_SLABS)], axis=1)
    ca_ref[...] = _ln_swish(conv, lng_ref[...], lnb_ref[...]).astype(BF16)


def _conva_sample(proj2d, state_planes, steps, dw_w, dw_b, ln_g, ln_b):
    depth, _, nreq, _ = state_planes.shape
    r = CONVA_SR
    vec = pl.BlockSpec((1, C_CONV), lambda i: (0, 0))
    st_spec = pl.BlockSpec((depth, CONF_HALO, r, C_CONV), lambda i: (0, 0, i, 0))
    slab = pltpu.VMEM((CONF_SLABS, r * steps, LANES), F32)
    return pl.pallas_call(
        _conva_sample_kernel,
        grid=(nreq // r,),
        in_specs=[
            pl.BlockSpec((r * steps, C_CONV), lambda i: (i, 0)),
            pl.BlockSpec((r * steps, C_CONV), lambda i: (i, 1)),
            st_spec,
            pl.BlockSpec((CONF_KERNEL, C_CONV), lambda i: (0, 0)),
            vec, vec, vec,
        ],
        out_specs=[pl.BlockSpec((r * steps, C_CONV), lambda i: (i, 0)), st_spec],
        out_shape=[
            jax.ShapeDtypeStruct((nreq * steps, C_CONV), BF16),
            jax.ShapeDtypeStruct(state_planes.shape, F32),
        ],
        scratch_shapes=[slab, slab],
        compiler_params=_params(1),
        name="conva_sample",
    )(proj2d, proj2d, state_planes, dw_w, dw_b, ln_g, ln_b)


SSD_SR = 8
SSD_ST = 8
SSD_SROWS = 16


def _ssd_sample_kernel(z_ref, xr_ref, bcr_ref, dt_ref, cst_ref, h0_ref, cw_ref, cb_ref, alog_ref,
                       dexp_ref, ng_ref, yn_ref, ncv_ref, h_ref, xe_all, dt8, z8):
    nreq = cst_ref.shape[2]
    steps = z_ref.shape[0] // nreq
    t = SSD_ST
    xe_all[...] = jnp.zeros(xe_all.shape, F32)
    dt8[...] = jnp.zeros(dt8.shape, F32)
    z8[...] = jnp.zeros(z8.shape, F32)
    rows = lax.broadcasted_iota(jnp.int32, (t, 1), 0)
    live = rows < steps
    lo_half = lax.broadcasted_iota(jnp.int32, (t, LANES), 1) < HEADDIM
    a_neg = -jnp.exp(alog_ref[...])
    groups = range(N_GROUPS)

    def request_phases(r, slot):
        rr = slice(r * steps, (r + 1) * steps)
        xs_ = xe_all.at[slot]
        v = {}

        def conv(j):
            acc = jnp.broadcast_to(cb_ref[:, _lanes(j)], (t, LANES))
            for k in range(SSM_CONV):
                acc = acc + cw_ref[k:k + 1, _lanes(j)] * xs_[j, k:k + t, :]
            return jnp.where(live, _silu(acc), 0.0)

        def p_load():
            xr = xr_ref[rr, :]
            bcr = bcr_ref[rr, :]
            for j in range(SSM_SLABS):
                for h in range(SSM_HALO):
                    xs_[j, h:h + 1, :] = cst_ref[0, h, r:r + 1, _lanes(j)]
                new = xr[:, _lanes(j)] if j < X_SLABS else bcr[:, _lanes(j - X_SLABS)]
                xs_[j, SSM_HALO:SSM_HALO + steps, :] = new
                for h in range(SSM_HALO):
                    ncv_ref[0, h, r:r + 1, _lanes(j)] = xs_[j, steps + h:steps + h + 1, :]
            dt8[slot, 0:steps, :] = dt_ref[rr, :]
            z8[slot, 0:steps, :] = z_ref[rr, :]

        def p_conv():
            dt = dt8[slot]
            a = dt * a_neg
            acum = jnp.zeros((t, DT_PAD), F32)
            for s in range(steps):
                acum = acum + jnp.where(rows >= s, a[s:s + 1, :], 0.0)
            tiles = [conv(j) for j in range(SSM_SLABS)]
            v.update(dt=dt, acum=acum, ealast=jnp.exp(acum[t - 1:t, :]),
                     xs=[jnp.concatenate([tiles[2 * g], tiles[2 * g + 1]], axis=1) for g in groups],
                     bm=[tiles[X_SLABS + g] for g in groups],
                     cm=[tiles[X_SLABS + N_GROUPS + g] for g in groups])

        def p_small_products():
            v["scores"] = [_dot_nt(v["cm"][g], v["bm"][g]) for g in groups]
            v["yoff"] = [_dot_nt(v["cm"][g], h0_ref[0, r, 4 * g:4 * g + 4].reshape(GROUP_W, D_STATE))
                         for g in groups]

        def p_spread():
            v["acx"] = [_expand_heads(v["acum"], g, lo_half) for g in groups]
            v["xdt"] = [v["xs"][g] * _expand_heads(v["dt"], g, lo_half) for g in groups]

        def p_vector():
            ys = []
            for g in groups:
                gs = slice(g * GROUP_W, (g + 1) * GROUP_W)
                acx, xdt, scores = v["acx"][g], v["xdt"][g], v["scores"][g]
                y = v["yoff"][g] * jnp.exp(acx) + v["xs"][g] * dexp_ref[:, gs]
                for s in range(steps):
                    decay = jnp.exp(jnp.where(rows >= s, acx - acx[s:s + 1, :], -jnp.inf))
                    y = y + (scores[:, s:s + 1] * decay) * xdt[s:s + 1, :]
                ys.append(y)
            v["ys"] = ys

        def p_state_products():
            v["upd"] = [_dot_tn(v["xdt"][g] * jnp.exp(v["acx"][g][t - 1:t, :] - v["acx"][g]), v["bm"][g])
                        for g in groups]

        def p_state_store():
            for g in groups:
                for e in range(HEADS_PER_GROUP):
                    hd = 4 * g + e
                    h_ref[0, r, hd] = (h0_ref[0, r, hd] * v["ealast"][:, hd:hd + 1]
                                       + v["upd"][g][e * HEADDIM:(e + 1) * HEADDIM, :])

        def p_out():
            for g in groups:
                gs = slice(g * GROUP_W, (g + 1) * GROUP_W)
                yn = _gated_group_norm(v["ys"][g], z8[slot, :, gs], ng_ref[:, gs])
                yn_ref[rr, gs] = yn[0:steps, :]

        return [p_load, p_conv, p_small_products, p_spread, p_vector, p_state_products, p_state_store, p_out]

    for pair in range(nreq // 2):
        for pa, pb in zip(request_phases(2 * pair, 0), request_phases(2 * pair + 1, 1)):
            pa()
            pb()


def _ssd_sample(proj2d, dt2d, conv_state, h0, steps, conv_w, conv_b, a_log, d_exp, norm_g):
    depth, nreq = conv_state.shape[0], conv_state.shape[2]
    r = SSD_SR
    cst_spec = pl.BlockSpec((depth, SSM_HALO, r, CONV_DIM), lambda i: (0, 0, i, 0))
    h_spec = pl.BlockSpec((depth, r, N_HEADS, HEADDIM, D_STATE), lambda i: (0, i, 0, 0, 0))
    return pl.pallas_call(
        _ssd_sample_kernel,
        grid=(nreq // r,),
        in_specs=[
            pl.BlockSpec((r * steps, D_INNER), lambda i: (i, 2)),
            pl.BlockSpec((r * steps, D_INNER), lambda i: (i, 3)),
            pl.BlockSpec((r * steps, D_INNER), lambda i: (i, 4)),
            pl.BlockSpec((r * steps, DT_PAD), lambda i: (i, 0)),
            cst_spec,
            h_spec,
            pl.BlockSpec((SSM_CONV, CONV_DIM), lambda i: (0, 0)),
            pl.BlockSpec((1, CONV_DIM), lambda i: (0, 0)),
            pl.BlockSpec((1, DT_PAD), lambda i: (0, 0)),
            pl.BlockSpec((1, D_INNER), lambda i: (0, 0)),
            pl.BlockSpec((1, D_INNER), lambda i: (0, 0)),
        ],
        out_specs=[pl.BlockSpec((r * steps, D_INNER), lambda i: (i, 0)), cst_spec, h_spec],
        out_shape=[
            jax.ShapeDtypeStruct((nreq * steps, D_INNER), F32),
            jax.ShapeDtypeStruct(conv_state.shape, F32),
            jax.ShapeDtypeStruct(h0.shape, F32),
        ],
        scratch_shapes=[
            pltpu.VMEM((2, SSM_SLABS, SSD_SROWS, LANES), F32),
            pltpu.VMEM((2, SSD_ST, DT_PAD), F32),
            pltpu.VMEM((2, SSD_ST, D_INNER), F32),
        ],
        compiler_params=_params(1),
        name="ssd_sample",
    )(proj2d, proj2d, proj2d, dt2d, conv_state, h0, conv_w, conv_b, a_log, d_exp, norm_g)


def _tail_sample_kernel(x_ref, ca_ref, yn_ref, gate_ref, wpw_ref, bpw_ref, wssm_ref, wout_ref, gffn_ref,
                        wup_ref, wdown_ref, gfin_ref, y_ref):
    branch_a = _dot(ca_ref[...], wpw_ref[...]) + bpw_ref[...]
    branch_b = _dot(yn_ref[...].astype(BF16), wssm_ref[...])
    merged = (_sigmoid(gate_ref[:, 0:D_MODEL]) * branch_a
              + _sigmoid(gate_ref[:, D_MODEL:2 * D_MODEL]) * branch_b)
    h = x_ref[...] + _dot(merged.astype(BF16), wout_ref[...])
    y_ref[...] = _mlp_final(h, gffn_ref, wup_ref, wdown_ref, gfin_ref)


def _tail_sample(x2d, ca, yn, proj2d, w_pw, b_pw, w_ssm, w_out, g_ffn, w_up, w_down, g_final):
    m = x2d.shape[0]
    vec = _const_spec((1, D_MODEL))
    return pl.pallas_call(
        _tail_sample_kernel,
        grid=(1,),
        in_specs=[
            pl.BlockSpec((m, D_MODEL), lambda i: (0, 0)),
            pl.BlockSpec((m, C_CONV), lambda i: (0, 0)),
            pl.BlockSpec((m, D_INNER), lambda i: (0, 0)),
            pl.BlockSpec((m, 2 * D_MODEL), lambda i: (0, 1)),
            _const_spec((C_CONV, D_MODEL)), vec, _const_spec((D_INNER, D_MODEL)),
            _const_spec((D_MODEL, D_MODEL)), vec, _const_spec((D_MODEL, D_FF)),
            _const_spec((D_FF, D_MODEL)), vec,
        ],
        out_specs=pl.BlockSpec((m, D_MODEL), lambda i: (0, 0)),
        out_shape=jax.ShapeDtypeStruct((m, D_MODEL), F32),
        compiler_params=_params(1),
        name="tail_sample",
    )(x2d, ca, yn, proj2d, w_pw, b_pw, w_ssm, w_out, g_ffn, w_up, w_down, g_final)


def kernel(x_prompt, x_sample, state_conf_conv, state_ssm_conv, state_ssm, g_mix, w_in, conf_dw_w,
           conf_dw_b, conf_ln_g, conf_ln_b, conf_w_pw, conf_b_pw, ssm_conv_w, ssm_conv_b, ssm_dt_bias,
           ssm_a_log, ssm_d, ssm_norm_g, ssm_w_out, w_out, g_ffn, w_up, w_down, g_final):
    depth = w_in.shape[0]
    assert depth == 1
    nb, seq, _ = x_prompt.shape
    nreq, steps, _ = x_sample.shape
    i = 0

    def row(v):
        return v.reshape(1, -1)

    def pad_lanes(v, width):
        return jnp.pad(v, ((0, 0), (0, width - v.shape[1])))

    gm = row(g_mix[i])
    w_main = w_in[i].astype(BF16)
    w_dt = pad_lanes(w_main[:, MAIN_COLS:], DT_PAD)
    b_dt = pad_lanes(row(ssm_dt_bias[i]), DT_PAD)
    a_log = pad_lanes(row(ssm_a_log[i]), DT_PAD)
    d_exp = row(jnp.repeat(ssm_d[i], HEADDIM))
    norm_g = row(ssm_norm_g[i])
    dw_w = conf_dw_w[i]
    dw_b, ln_g, ln_b = row(conf_dw_b[i]), row(conf_ln_g[i]), row(conf_ln_b[i])
    cw, cb = ssm_conv_w[i], row(ssm_conv_b[i])
    w_pw, b_pw = conf_w_pw[i].astype(BF16), row(conf_b_pw[i])
    w_ssm = ssm_w_out[i].astype(BF16)
    mlp_w = (w_out[i].astype(BF16), row(g_ffn[i]), w_up[i].astype(BF16), w_down[i].astype(BF16),
             row(g_final))

    ga, conf_p = _conva_prompt(x_prompt, gm, w_main, dw_w, dw_b, ln_g, ln_b, w_pw, b_pw)
    head_row = jnp.arange(3 * DT_PAD, dtype=jnp.int32)[:, None] % DT_PAD
    ex = (head_row == jnp.arange(D_INNER, dtype=jnp.int32)[None, :] // HEADDIM).astype(BF16)
    gb, scv_p, h_p = _ssd_prompt(x_prompt, gm, w_main, w_dt, b_dt, w_ssm, cw, cb, a_log, d_exp, norm_g, ex)
    m_p = nb * seq
    y_p = _tail_prompt(x_prompt.reshape(m_p, D_MODEL), ga.reshape(m_p, D_MODEL), gb.reshape(m_p, D_MODEL),
                       *mlp_w)

    xs = x_sample.reshape(nreq * steps, D_MODEL)
    proj_s, dt_s = _inproj(xs, gm, w_main, w_dt, b_dt)
    planes = (0, 2, 1, 3)
    ca_s, conf_s = _conva_sample(proj_s, state_conf_conv.transpose(planes), steps, dw_w, dw_b, ln_g, ln_b)
    yn_s, scv_s, h_s = _ssd_sample(proj_s, dt_s, state_ssm_conv.transpose(planes), state_ssm, steps, cw, cb,
                                   a_log, d_exp, norm_g)
    y_s = _tail_sample(xs, ca_s, yn_s, proj_s, w_pw, b_pw, w_ssm, *mlp_w)

    return (y_p.reshape(nb, seq, D_MODEL), y_s.reshape(nreq, steps, D_MODEL),
            conf_p[None], scv_p[None], h_p[None], conf_s.transpose(planes), scv_s.transpose(planes), h_s)
```
